```python
import jax, jax.numpy as jnp
from jax import lax
import numpy as np

D_MODEL = 1024
BATCH = 32
SEQ = 256
DEPTH = 4
DEC_BATCH = 4
DEC_SEQ = 4096
PAST_LEN = 256

GRID_W = 64
N_MOD = 9
D_FF = 2816
H_MLA = 8
QK_NOPE = 64
ROPE_DIM = 32
V_HD = 64
Q_LORA = 256
KV_LORA = 128
D_MLA = H_MLA * V_HD
MLA_SCALE = (QK_NOPE + ROPE_DIM) ** -0.5
H_RWKV = 4
RWKV_HD = 64
D_RWKV = H_RWKV * RWKV_HD
LORA_WA = 64
LORA_G = 128
RWKV_COLS = 3 * D_RWKV + 4 * LORA_WA + LORA_G
DECAY_SCALE = 0.6065306597126334
RWKV_GN_EPS = 64e-5
H_RET = 4
RET_DK = 64
RET_DV = 64
D_RET = H_RET * RET_DV
RET_CHUNK = 128
RET_COLS = 2 * H_RET * RET_DK + 3 * D_RET
RET_GN_EPS = 1e-5
D_MIX = D_MLA + D_RWKV + D_RET
IN_COLS = Q_LORA + KV_LORA + ROPE_DIM + RWKV_COLS + RET_COLS
ROPE_BASE = 10000.0
Q_BLOCK = 128
NORM_EPS = 1e-6

kernel_name = 'hybrid_mla_rwkv7_retention_prefix_dit_step'


def split_cols(x, widths):
    parts, off = [], 0
    for w in widths:
        parts.append(x[..., off:off + w])
        off += w
    return parts


def rmsnorm(x, g):
    xf = x.astype(jnp.float32)
    y = xf * lax.rsqrt(jnp.mean(xf * xf, axis=-1, keepdims=True) + NORM_EPS)
    return (y * g.astype(jnp.float32)).astype(x.dtype)


def head_groupnorm(y, g, eps):
    yc = y - jnp.mean(y, axis=-1, keepdims=True)
    return yc * lax.rsqrt(jnp.mean(yc * yc, axis=-1, keepdims=True) + eps) * g


def rope_1d(x, pos):
    half = x.shape[-1] // 2
    inv = ROPE_BASE ** (-jnp.arange(half, dtype=jnp.float32) / half)
    ang = pos.astype(jnp.float32)[:, None] * inv
    cos, sin = jnp.cos(ang)[:, None, :], jnp.sin(ang)[:, None, :]
    x1, x2 = x[..., :half], x[..., half:]
    return jnp.concatenate([x1 * cos - x2 * sin, x1 * sin + x2 * cos], axis=-1)


def axial_rope(x, row, col):
    xf = x.astype(jnp.float32)
    d = x.shape[-1] // 2
    return jnp.concatenate([rope_1d(xf[..., :d], row), rope_1d(xf[..., d:], col)], axis=-1).astype(x.dtype)


def centred_shift(x, mu):
    xp = jnp.pad(x, ((0, 0), (1, 1), (0, 0)))
    return x + mu * (0.5 * (xp[:, :-2] + xp[:, 2:]) - x)


def both_dirs(t):
    return jnp.broadcast_to(t[:, None], (t.shape[0], 2) + t.shape[1:])


def to_scan_order(t):
    return jnp.concatenate([t[:, :1], jnp.flip(t[:, 1:], axis=2)], axis=1)


def swiglu(h, wi, wo):
    gu = h @ wi
    return (jax.nn.silu(gu[..., :D_FF]) * gu[..., D_FF:]) @ wo


def attend(q, k, v, scale):
    B, N, H, dk = q.shape
    nb = N // Q_BLOCK
    qb = jnp.moveaxis(q.reshape(B, nb, Q_BLOCK, H, dk), 1, 0)

    def block(qi):
        s = jnp.einsum('bqhd,bkhd->bhqk', qi, k).astype(jnp.float32) * scale
        p = jax.nn.softmax(s, axis=-1).astype(v.dtype)
        return jnp.einsum('bhqk,bkhd->bqhd', p, v)

    o = lax.map(block, qb)
    return jnp.moveaxis(o, 0, 1).reshape(B, N, H, v.shape[-1])


def mla_expand(ckv, krope_h, wkv_up):
    B, M, _ = ckv.shape
    kv = (ckv @ wkv_up).reshape(B, M, H_MLA, QK_NOPE + V_HD)
    k = jnp.concatenate([kv[..., :QK_NOPE], jnp.broadcast_to(krope_h, (B, M, H_MLA, ROPE_DIM))], axis=-1)
    return k, kv[..., QK_NOPE:]


def rwkv7_scan(r, w, kk, kka, k, v, s0):
    def step(S, inp):
        r_t, w_t, kk_t, kka_t, k_t, v_t = inp
        sa = jnp.einsum('bzhvk,bzhk->bzhv', S, -kk_t)
        S = S * w_t[..., None, :] + sa[..., None] * kka_t[..., None, :] + v_t[..., None] * k_t[..., None, :]
        return S, jnp.einsum('bzhvk,bzhk->bzhv', S, r_t)

    xs = tuple(jnp.moveaxis(t, 2, 0) for t in (r, w, kk, kka, k, v))
    s_fin, ys = lax.scan(step, s0, xs)
    return jnp.moveaxis(ys, 0, 2), s_fin


def retention_log_decay():
    e = 5.0 + jnp.arange(H_RET, dtype=jnp.float32)[None, :] + 0.5 * jnp.arange(2, dtype=jnp.float32)[:, None]
    return jnp.log1p(-jnp.exp2(-e))


def retention_chunkwise(q, k, v, s0, log_gamma):
    B, Z, N, H, dk = q.shape
    dv = v.shape[-1]
    C = RET_CHUNK
    nc = N // C
    q = q.reshape(B, Z, nc, C, H, dk)
    k = k.reshape(B, Z, nc, C, H, dk)
    v = v.reshape(B, Z, nc, C, H, dv)
    idx = jnp.arange(C, dtype=jnp.float32)
    lg = log_gamma[:, :, None]
    diff = idx[:, None] - idx[None, :]
    dmask = jnp.where(diff >= 0, jnp.exp(lg[..., None] * jnp.maximum(diff, 0.0)), 0.0)
    scores = jnp.einsum('bznihd,bznjhd->bzhnij', q, k) * dmask[None, :, :, None]
    inner = jnp.einsum('bzhnij,bznjhe->bznihe', scores, v)
    zeta = jnp.exp(lg * (C - 1.0 - idx))
    xi = jnp.exp(lg * (idx + 1.0))
    kv_chunk = jnp.einsum('bznjhd,zhj,bznjhe->nbzhde', k, zeta, v)
    g_chunk = jnp.exp(log_gamma * C)[:, :, None, None]

    def step(R, kvc):
        return R * g_chunk + kvc, R

    s_fin, r_before = lax.scan(step, s0, kv_chunk)
    cross = jnp.einsum('bznihd,zhi,nbzhde->bznihe', q, xi, r_before)
    return (inner + cross).reshape(B, Z, N, H, dv), s_fin


def token_mixer(h, lw, ctx, pos):
    B, N, _ = h.shape
    f32 = jnp.float32
    qa, kva, krope, rw, rt = split_cols(h @ lw['w_in'], [Q_LORA, KV_LORA, ROPE_DIM, RWKV_COLS, RET_COLS])

    q = (rmsnorm(qa, lw['mla_norm_q']) @ lw['mla_wq_up']).reshape(B, N, H_MLA, QK_NOPE + ROPE_DIM)
    q_nope, q_rope = q[..., :QK_NOPE], q[..., QK_NOPE:]
    ckv = rmsnorm(kva, lw['mla_norm_kv'])
    krope_h = krope[:, :, None, :]
    if pos is not None:
        q_rope = axial_rope(q_rope, pos[0], pos[1])
        krope_h = axial_rope(krope_h, pos[0], pos[1])
    k, v = mla_expand(ckv, krope_h, lw['mla_wkv_up'])
    if ctx is not None:
        kc, vc = mla_expand(ctx[0], ctx[1][:, :, None, :], lw['mla_wkv_up'])
        k = jnp.concatenate([kc, k], axis=1)
        v = jnp.concatenate([vc, v], axis=1)
    mla_o = attend(jnp.concatenate([q_nope, q_rope], axis=-1), k, v, MLA_SCALE).reshape(B, N, D_MLA)

    xr = centred_shift(rw, lw['rwkv_mu'])
    r, kr, vr, zwf, zwb, zaf, zab, zg = split_cols(xr, [D_RWKV] * 3 + [LORA_WA] * 4 + [LORA_G])
    vec = lw['rwkv_vec'].astype(f32)
    up = lw['rwkv_lora_up']
    zw = jnp.stack([jnp.tanh(zwf) @ up[0], jnp.tanh(zwb) @ up[1]], axis=1).astype(f32) + vec[None, 0:2, None]
    decay = jnp.exp(-DECAY_SCALE * jax.nn.sigmoid(zw))
    a = jax.nn.sigmoid(jnp.stack([zaf @ up[2], zab @ up[3]], axis=1).astype(f32) + vec[None, 2:4, None])
    gate = (jax.nn.sigmoid(zg) @ lw['rwkv_g_up']).astype(f32)
    r, kr, vr = r.astype(f32), kr.astype(f32), vr.astype(f32)
    kk = (kr * vec[4]).reshape(B, N, H_RWKV, RWKV_HD)
    kk = kk / (jnp.sqrt(jnp.sum(kk * kk, axis=-1, keepdims=True)) + 1e-12)
    kd = kr[:, None] * (1.0 + (a - 1.0) * vec[5])
    heads2 = lambda t: t.reshape(B, 2, N, H_RWKV, RWKV_HD)
    rh = both_dirs(r.reshape(B, N, H_RWKV, RWKV_HD))
    vh = both_dirs(vr.reshape(B, N, H_RWKV, RWKV_HD))
    kkh = both_dirs(kk)
    ah, wh, kh = heads2(a), heads2(decay), heads2(kd)
    s0 = jnp.zeros((B, 2, H_RWKV, RWKV_HD, RWKV_HD), f32) if ctx is None else ctx[2].astype(f32)
    y, s_rwkv = rwkv7_scan(to_scan_order(rh), to_scan_order(wh), to_scan_order(kkh),
                           to_scan_order(kkh * ah), to_scan_order(kh), to_scan_order(vh), s0)
    y = to_scan_order(y)
    bonus = jnp.sum(rh * kh * vec[6].reshape(H_RWKV, RWKV_HD), axis=-1, keepdims=True) * vh
    y = head_groupnorm(y, vec[7].reshape(H_RWKV, RWKV_HD), RWKV_GN_EPS) + bonus
    rwkv_o = (jnp.sum(y, axis=1).reshape(B, N, D_RWKV) * gate).astype(h.dtype)

    qr, kr2, vr2, gf, gb = split_cols(rt, [H_RET * RET_DK] * 2 + [D_RET] * 3)
    qh = qr.reshape(B, N, H_RET, RET_DK)
    kh2 = kr2.reshape(B, N, H_RET, RET_DK)
    if pos is not None:
        qh = axial_rope(qh, pos[0], pos[1])
        kh2 = axial_rope(kh2, pos[0], pos[1])
    q2 = to_scan_order(both_dirs(qh.astype(f32)))
    k2 = to_scan_order(both_dirs(kh2.astype(f32) * RET_DK ** -0.5))
    v2 = to_scan_order(both_dirs(vr2.astype(f32).reshape(B, N, H_RET, RET_DV)))
    s0r = jnp.zeros((B, 2, H_RET, RET_DK, RET_DV), f32) if ctx is None else ctx[3].astype(f32)
    o, s_ret = retention_chunkwise(q2, k2, v2, s0r, retention_log_decay())
    o = head_groupnorm(to_scan_order(o), lw['ret_gn'].astype(f32).reshape(1, 2, 1, H_RET, RET_DV), RET_GN_EPS)
    g2 = jax.nn.silu(jnp.stack([gf, gb], axis=1).astype(f32)).reshape(B, 2, N, H_RET, RET_DV)
    ret_o = jnp.sum(o * g2, axis=1).reshape(B, N, D_RET).astype(h.dtype)

    mixed = jnp.concatenate([mla_o, rwkv_o, ret_o], axis=-1) @ lw['w_out']
    return mixed, (ckv, krope, s_rwkv, s_ret)


def trunk_layer(x, mod, lw, ctx, pos):
    ng = lw['norm_g']
    m = lambda i: mod[:, i][:, None, :]
    h = rmsnorm(x, ng[0]) * (1.0 + m(1)) + m(0)
    x = x + 0.5 * m(2) * rmsnorm(swiglu(h, lw['ffn_wi'][0], lw['ffn_wo'][0]), ng[1])
    h = rmsnorm(x, ng[2]) * (1.0 + m(4)) + m(3)
    mixed, ctx_out = token_mixer(h, lw, ctx, pos)
    x = x + m(5) * rmsnorm(mixed, ng[3])
    h = rmsnorm(x, ng[4]) * (1.0 + m(7)) + m(6)
    x = x + 0.5 * m(8) * rmsnorm(swiglu(h, lw['ffn_wi'][1], lw['ffn_wo'][1]), ng[5])
    return x, ctx_out


def setup_inputs(seed: int = 0) -> dict:
    key = jax.random.key(seed)
    keys = jax.random.split(key, 40)
    counter = [0]

    def nk():
        kk = keys[counter[0]]
        counter[0] += 1
        return kk

    def nrm(shape, s=1.0):
        return s * jax.random.normal(nk(), shape, jnp.float32)

    def gain(shape):
        return 1.0 + nrm(shape, 0.05)

    rwkv_off = jnp.array([0.0, 0.0, 0.0, 0.0, 0.85, 1.0, 0.0, 1.0], jnp.float32)[None, :, None]
    rwkv_sc = jnp.array([0.5, 0.5, 0.5, 0.5, 0.05, 0.05, 0.1, 0.05], jnp.float32)[None, :, None]
    return {
        'x_prompt': nrm((BATCH, SEQ, D_MODEL)),
        'x_sample': nrm((DEC_BATCH, DEC_SEQ, D_MODEL)),
        'cache_mla_ckv': nrm((DEC_BATCH, DEPTH, PAST_LEN, KV_LORA)),
        'cache_mla_krope': nrm((DEC_BATCH, DEPTH, PAST_LEN, ROPE_DIM)),
        'state_rwkv': nrm((DEC_BATCH, DEPTH, 2, H_RWKV, RWKV_HD, RWKV_HD), 0.5),
        'state_ret': nrm((DEC_BATCH, DEPTH, 2, H_RET, RET_DK, RET_DV), 0.5),
        'c': nrm((DEC_BATCH, D_MODEL)),
        'c_ctx': nrm((D_MODEL,)),
        'norm_g': gain((DEPTH, 6, D_MODEL)),
        'w_mod': nrm((DEPTH, D_MODEL, N_MOD * D_MODEL), D_MODEL ** -0.5),
        'b_mod': nrm((DEPTH, N_MOD * D_MODEL), 0.02),
        'ffn_wi': nrm((DEPTH, 2, D_MODEL, 2 * D_FF), D_MODEL ** -0.5),
        'ffn_wo': nrm((DEPTH, 2, D_FF, D_MODEL), D_FF ** -0.5),
        'w_in': nrm((DEPTH, D_MODEL, IN_COLS), D_MODEL ** -0.5),
        'w_out': nrm((DEPTH, D_MIX, D_MODEL), D_MIX ** -0.5),
        'mla_norm_q': gain((DEPTH, Q_LORA)),
        'mla_norm_kv': gain((DEPTH, KV_LORA)),
        'mla_wq_up': nrm((DEPTH, Q_LORA, H_MLA * (QK_NOPE + ROPE_DIM)), Q_LORA ** -0.5),
        'mla_wkv_up': nrm((DEPTH, KV_LORA, H_MLA * (QK_NOPE + V_HD)), KV_LORA ** -0.5),
        'rwkv_mu': jax.random.uniform(nk(), (DEPTH, RWKV_COLS), jnp.float32),
        'rwkv_vec': rwkv_off + rwkv_sc * nrm((DEPTH, 8, D_RWKV)),
        'rwkv_lora_up': nrm((DEPTH, 4, LORA_WA, D_RWKV), 0.5 * LORA_WA ** -0.5),
        'rwkv_g_up': nrm((DEPTH, LORA_G, D_RWKV), LORA_G ** -0.5),
        'ret_gn': gain((DEPTH, 2, D_RET)),
    }


def reference(x_prompt, x_sample, cache_mla_ckv, cache_mla_krope, state_rwkv, state_ret, c, c_ctx,
              norm_g, w_mod, b_mod, ffn_wi, ffn_wo, w_in, w_out, mla_norm_q, mla_norm_kv, mla_wq_up,
              mla_wkv_up, rwkv_mu, rwkv_vec, rwkv_lora_up, rwkv_g_up, ret_gn):
    n_lat = x_sample.shape[1]
    ROWS = n_lat // GRID_W
    row = jnp.repeat(jnp.arange(ROWS), GRID_W)
    col = jnp.tile(jnp.arange(GRID_W), ROWS)
    silu_ctx = jax.nn.silu(c_ctx)[None, :]
    silu_c = jax.nn.silu(c)
    yp, ys = x_prompt, x_sample
    ckv_l, krope_l, rwkv_l, ret_l = [], [], [], []
    for l in range(DEPTH):
        lw = {'norm_g': norm_g[l], 'ffn_wi': ffn_wi[l], 'ffn_wo': ffn_wo[l], 'w_in': w_in[l],
              'w_out': w_out[l], 'mla_norm_q': mla_norm_q[l], 'mla_norm_kv': mla_norm_kv[l],
              'mla_wq_up': mla_wq_up[l], 'mla_wkv_up': mla_wkv_up[l], 'rwkv_mu': rwkv_mu[l],
              'rwkv_vec': rwkv_vec[l], 'rwkv_lora_up': rwkv_lora_up[l], 'rwkv_g_up': rwkv_g_up[l],
              'ret_gn': ret_gn[l]}
        mod_ctx = (silu_ctx @ w_mod[l] + b_mod[l]).reshape(1, N_MOD, D_MODEL)
        mod_lat = (silu_c @ w_mod[l] + b_mod[l]).reshape(-1, N_MOD, D_MODEL)
        yp, (ckv, krope, s_rwkv, s_ret) = trunk_layer(yp, mod_ctx, lw, None, None)
        ckv_l.append(ckv)
        krope_l.append(krope)
        rwkv_l.append(s_rwkv.astype(x_prompt.dtype))
        ret_l.append(s_ret.astype(x_prompt.dtype))
        ctx = (cache_mla_ckv[:, l], cache_mla_krope[:, l], state_rwkv[:, l], state_ret[:, l])
        ys, _ = trunk_layer(ys, mod_lat, lw, ctx, (row, col))
    new_mla_ckv = jnp.stack(ckv_l, axis=1)
    new_mla_krope = jnp.stack(krope_l, axis=1)
    new_rwkv = jnp.stack(rwkv_l, axis=1)
    new_ret = jnp.stack(ret_l, axis=1)
    return (yp, ys, new_mla_ckv, new_mla_krope, new_rwkv, new_ret)
```

```python
import functools

import numpy as np
import jax
import jax.numpy as jnp
from jax import lax
from jax.experimental import pallas as pl
from jax.experimental.pallas import tpu as pltpu

F32 = jnp.float32
BF16 = jnp.bfloat16
HIGHEST = lax.Precision.HIGHEST

D_MODEL = 1024
N_MOD = 9
D_FF = 2816
H_MLA = 8
QK_NOPE = 64
ROPE_DIM = 32
V_HD = 64
Q_LORA = 256
KV_LORA = 128
D_MLA = H_MLA * V_HD
MLA_SCALE = (QK_NOPE + ROPE_DIM) ** -0.5
H_RWKV = 4
RWKV_HD = 64
D_RWKV = H_RWKV * RWKV_HD
LORA_WA = 64
LORA_G = 128
DECAY_SCALE = 0.6065306597126334
RWKV_GN_EPS = 64e-5
H_RET = 4
RET_DK = 64
D_RET = H_RET * 64
RET_GN_EPS = 1e-5
ROPE_BASE = 10000.0
NORM_EPS = 1e-6
GRID_W = 64

LANE = 128
HEAD_PAD = 128
ROPE_OFF = QK_NOPE
CHUNK = 64
STACK = H_RWKV * CHUNK
VMEM_LIMIT = 56 * 1024 * 1024

C_QA = 0
C_KVA = C_QA + Q_LORA
C_KR = C_KVA + KV_LORA
C_KRS = C_KR + LANE
C_RW = C_KRS + LANE
RW_W = 3 * D_RWKV + 4 * LANE + LORA_G
C_RT = C_RW + RW_W
RT_W = 2 * H_RET * RET_DK + 3 * D_RET
C_RTS = C_RT + RT_W
RTS_W = 2 * H_RET * RET_DK
IN_W = C_RTS + RTS_W


def _params(*sem):
    return pltpu.CompilerParams(dimension_semantics=sem, vmem_limit_bytes=VMEM_LIMIT)


def _sigmoid(x):
    return 1.0 / (1.0 + jnp.exp(-x))


def _rms(x, g):
    return x * lax.rsqrt(jnp.mean(x * x, axis=-1, keepdims=True) + NORM_EPS) * g


def _bdot(a, b):
    return jnp.dot(a.astype(BF16), b.astype(BF16), preferred_element_type=F32)


def _bdot_nt(a, b):
    return lax.dot_general(a.astype(BF16), b.astype(BF16), (((1,), (1,)), ((), ())),
                           preferred_element_type=F32)


def _fdot(a, b):
    return jnp.dot(a, b, precision=HIGHEST, preferred_element_type=F32)


def _fdot_nt(a, b):
    return lax.dot_general(a, b, (((1,), (1,)), ((), ())), precision=HIGHEST,
                           preferred_element_type=F32)


def _fdot_tn(a, b):
    return lax.dot_general(a, b, (((0,), (0,)), ((), ())), precision=HIGHEST,
                           preferred_element_type=F32)


def _bdot_tn(a, b):
    return lax.dot_general(a.astype(BF16), b.astype(BF16), (((0,), (0,)), ((), ())),
                           preferred_element_type=F32)


def _head_mean_matrix():
    r = lax.broadcasted_iota(jnp.int32, (STACK, STACK), 0) // RWKV_HD
    c = lax.broadcasted_iota(jnp.int32, (STACK, STACK), 1) // RWKV_HD
    return jnp.where(r == c, 1.0 / RWKV_HD, 0.0).astype(F32)


def _head_groupnorm(y, gain, eps):
    mm = _head_mean_matrix()
    yc = y - _fdot(y, mm)
    var = _fdot(yc * yc, mm)
    return yc * lax.rsqrt(var + eps) * gain


def _stack_heads(x):
    head = lax.broadcasted_iota(jnp.int32, x.shape, 1) // RWKV_HD
    return jnp.concatenate([jnp.where(head == h, x, 0.0) for h in range(H_RWKV)], axis=0)


def _unstack_heads(x):
    out = x[0:CHUNK]
    for h in range(1, H_RWKV):
        out = out + x[h * CHUNK:(h + 1) * CHUNK]
    return out


def _mod_body(c_ref, w_ref, b_ref, o_ref):
    c = c_ref[...]
    s = c * _sigmoid(c)
    o_ref[0] = _bdot(s, w_ref[0]) + b_ref[0]


def _modulation(c_all, w_mod, b_mod):
    depth, d, nd = w_mod.shape
    tn = D_MODEL
    return pl.pallas_call(
        _mod_body,
        grid=(depth, nd // tn),
        in_specs=[pl.BlockSpec((8, d), lambda l, j: (0, 0)),
                  pl.BlockSpec((1, d, tn), lambda l, j: (l, 0, j)),
                  pl.BlockSpec((1, 1, tn), lambda l, j: (l, 0, j))],
        out_specs=pl.BlockSpec((1, 8, tn), lambda l, j: (l, 0, j)),
        out_shape=jax.ShapeDtypeStruct((depth, 8, nd), F32),
        compiler_params=_params("parallel", "parallel"),
        name="modulation",
    )(c_all, w_mod, b_mod.reshape(depth, 1, nd))


def _ffn_body(x_ref, mod_ref, ng_ref, wg_ref, wu_ref, wo_ref, o_ref, h_ref, acc_ref, *, mrow, grow):
    f = pl.program_id(1)

    @pl.when(f == 0)
    def _():
        m = mod_ref[0]
        h = _rms(x_ref[...], ng_ref[grow:grow + 1, :]) * (1.0 + m[mrow + 1:mrow + 2, :]) + m[mrow:mrow + 1, :]
        h_ref[...] = h.astype(BF16)
        acc_ref[...] = jnp.zeros_like(acc_ref)

    h = h_ref[...]
    g = jnp.dot(h, wg_ref[...], preferred_element_type=F32)
    u = jnp.dot(h, wu_ref[...], preferred_element_type=F32)
    a = (g * _sigmoid(g)) * u
    acc_ref[...] += jnp.dot(a.astype(BF16), wo_ref[...], preferred_element_type=F32)

    @pl.when(f == pl.num_programs(1) - 1)
    def _():
        m = mod_ref[0]
        o_ref[...] = x_ref[...] + 0.5 * m[mrow + 2:mrow + 3, :] * _rms(acc_ref[...], ng_ref[grow + 1:grow + 2, :])


def _ffn(x, mod, ng, wi, wo, layer, which, grp, tm):
    t = x.shape[0]
    tf = 256
    nf = D_FF // tf
    body = functools.partial(_ffn_body, mrow=6 * which, grow=4 * which)
    return pl.pallas_call(
        body,
        grid=(t // tm, nf),
        in_specs=[pl.BlockSpec((tm, D_MODEL), lambda i, f: (i, 0)),
                  pl.BlockSpec((1, N_MOD, D_MODEL), lambda i, f: (grp(i), 0, 0)),
                  pl.BlockSpec((6, D_MODEL), lambda i, f: (0, 0)),
                  pl.BlockSpec((None, None, D_MODEL, tf), lambda i, f: (layer, which, 0, f)),
                  pl.BlockSpec((None, None, D_MODEL, tf), lambda i, f: (layer, which, 0, f + nf)),
                  pl.BlockSpec((None, None, tf, D_MODEL), lambda i, f: (layer, which, f, 0))],
        out_specs=pl.BlockSpec((tm, D_MODEL), lambda i, f: (i, 0)),
        out_shape=jax.ShapeDtypeStruct((t, D_MODEL), F32),
        scratch_shapes=[pltpu.VMEM((tm, D_MODEL), BF16), pltpu.VMEM((tm, D_MODEL), F32)],
        compiler_params=_params("parallel", "arbitrary"),
        name="ffn",
    )(x, mod, ng, wi, wi, wo)


def _mixin_body(x_ref, mod_ref, ng_ref, w_ref, qa_ref, kva_ref, kr_ref, krs_ref, rw_ref, rt_ref, rts_ref):
    m = mod_ref[0]
    h = (_rms(x_ref[...], ng_ref[2:3, :]) * (1.0 + m[4:5, :]) + m[3:4, :]).astype(BF16)

    def proj(lo, width):
        return jnp.dot(h, w_ref[:, lo:lo + width], preferred_element_type=F32)

    qa_ref[...] = proj(C_QA, Q_LORA)
    kva_ref[...] = proj(C_KVA, KV_LORA)
    kr_ref[...] = proj(C_KR, LANE)
    krs_ref[...] = proj(C_KRS, LANE)
    rw_ref[...] = proj(C_RW, RW_W)
    rt_ref[...] = proj(C_RT, RT_W)
    rts_ref[...] = proj(C_RTS, RTS_W)


def _mixin(x, mod, ng, w_in, grp, tm):
    t = x.shape[0]
    widths = (Q_LORA, KV_LORA, LANE, LANE, RW_W, RT_W, RTS_W)
    return pl.pallas_call(
        _mixin_body,
        grid=(t // tm,),
        in_specs=[pl.BlockSpec((tm, D_MODEL), lambda i: (i, 0)),
                  pl.BlockSpec((1, N_MOD, D_MODEL), lambda i: (grp(i), 0, 0)),
                  pl.BlockSpec((6, D_MODEL), lambda i: (0, 0)),
                  pl.BlockSpec((D_MODEL, IN_W), lambda i: (0, 0))],
        out_specs=[pl.BlockSpec((tm, w), lambda i: (i, 0)) for w in widths],
        out_shape=[jax.ShapeDtypeStruct((t, w), F32) for w in widths],
        compiler_params=_params("parallel"),
        name="mixer_in",
    )(x, mod, ng, w_in)


def _kv_expand(ckv, krr, wk_ref, wv_ref, kh_ref, v_ref):
    c = ckv.astype(BF16)
    kn = jnp.dot(c, wk_ref[...], preferred_element_type=F32)
    v = jnp.dot(c, wv_ref[...], preferred_element_type=F32)
    for h in range(H_MLA):
        kh_ref[h] = (kn[:, h * HEAD_PAD:(h + 1) * HEAD_PAD] + krr).astype(BF16)
    for p in range(H_MLA // 2):
        v_ref[p] = v[:, p * LANE:(p + 1) * LANE].astype(BF16)


def _mla_prep_body(qa_ref, kva_ref, kr_ref, krs_ref, cq_ref, sq_ref, gq_ref, gkv_ref, wq_ref, wqs_ref,
                   wk_ref, wv_ref, qh_ref, ckv_ref, kh_ref, v_ref):
    cq = cq_ref[...]
    sq = sq_ref[...]
    qn = _rms(qa_ref[...], gq_ref[...]).astype(BF16)
    q = jnp.dot(qn, wq_ref[...], preferred_element_type=F32)
    qs = jnp.dot(qn, wqs_ref[...], preferred_element_type=F32)
    for h in range(H_MLA):
        sl = slice(h * HEAD_PAD, (h + 1) * HEAD_PAD)
        qh_ref[h] = (q[:, sl] * cq + qs[:, sl] * sq).astype(BF16)
    ckv = _rms(kva_ref[...], gkv_ref[...])
    ckv_ref[...] = ckv
    krr = kr_ref[...] * cq + krs_ref[...] * sq
    _kv_expand(ckv, krr, wk_ref, wv_ref, kh_ref, v_ref)


def _mla_prep(qa, kva, kr, krs, cq, sq, gq, gkv, wq, wqs, wk, wv, tm):
    t = qa.shape[0]
    row = lambda w: pl.BlockSpec((tm, w), lambda i: (i, 0))
    full = lambda a: pl.BlockSpec(a.shape, lambda i: (0,) * a.ndim)
    return pl.pallas_call(
        _mla_prep_body,
        grid=(t // tm,),
        in_specs=[row(Q_LORA), row(KV_LORA), row(LANE), row(LANE), row(LANE), row(LANE),
                  full(gq), full(gkv), full(wq), full(wqs), full(wk), full(wv)],
        out_specs=[pl.BlockSpec((H_MLA, tm, HEAD_PAD), lambda i: (0, i, 0)),
                   row(KV_LORA),
                   pl.BlockSpec((H_MLA, tm, HEAD_PAD), lambda i: (0, i, 0)),
                   pl.BlockSpec((H_MLA // 2, tm, LANE), lambda i: (0, i, 0))],
        out_shape=[jax.ShapeDtypeStruct((H_MLA, t, HEAD_PAD), BF16),
                   jax.ShapeDtypeStruct((t, KV_LORA), F32),
                   jax.ShapeDtypeStruct((H_MLA, t, HEAD_PAD), BF16),
                   jax.ShapeDtypeStruct((H_MLA // 2, t, LANE), BF16)],
        compiler_params=_params("parallel"),
        name="mla_prep",
    )(qa, kva, kr, krs, cq, sq, gq, gkv, wq, wqs, wk, wv)


def _cache_kv_body(ckv_ref, kr_ref, wk_ref, wv_ref, kh_ref, v_ref):
    _kv_expand(ckv_ref[...], kr_ref[...], wk_ref, wv_ref, kh_ref, v_ref)


def _cache_kv(ckv, kr, wk, wv):
    t = ckv.shape[0]
    tm = min(t, 512)
    row = lambda w: pl.BlockSpec((tm, w), lambda i: (i, 0))
    full = lambda a: pl.BlockSpec(a.shape, lambda i: (0,) * a.ndim)
    return pl.pallas_call(
        _cache_kv_body,
        grid=(t // tm,),
        in_specs=[row(KV_LORA), row(LANE), full(wk), full(wv)],
        out_specs=[pl.BlockSpec((H_MLA, tm, HEAD_PAD), lambda i: (0, i, 0)),
                   pl.BlockSpec((H_MLA // 2, tm, LANE), lambda i: (0, i, 0))],
        out_shape=[jax.ShapeDtypeStruct((H_MLA, t, HEAD_PAD), BF16),
                   jax.ShapeDtypeStruct((H_MLA // 2, t, LANE), BF16)],
        compiler_params=_params("parallel"),
        name="cache_kv",
    )(ckv, kr, wk, wv)


def _attn_body(*refs, two):
    if two:
        q_ref, k1_ref, v1_ref, k2_ref, v2_ref, o_ref = refs
    else:
        q_ref, k1_ref, v1_ref, o_ref = refs
    outs = []
    for j in range(2):
        q = q_ref[j]
        s1 = lax.dot_general(q, k1_ref[j], (((1,), (1,)), ((), ())), preferred_element_type=F32) * MLA_SCALE
        m = jnp.max(s1, axis=-1, keepdims=True)
        if two:
            s2 = lax.dot_general(q, k2_ref[j], (((1,), (1,)), ((), ())), preferred_element_type=F32) * MLA_SCALE
            m = jnp.maximum(m, jnp.max(s2, axis=-1, keepdims=True))
        p1 = jnp.exp(s1 - m)
        l = jnp.sum(p1, axis=-1, keepdims=True)
        o = jnp.dot(p1.astype(BF16), v1_ref[0], preferred_element_type=F32)
        if two:
            p2 = jnp.exp(s2 - m)
            l = l + jnp.sum(p2, axis=-1, keepdims=True)
            o = o + jnp.dot(p2.astype(BF16), v2_ref[0], preferred_element_type=F32)
        outs.append(o / l)
    lane = lax.broadcasted_iota(jnp.int32, outs[0].shape, 1)
    o_ref[...] = jnp.where(lane < V_HD, outs[0], outs[1])


def _attention(qh, kh1, v1, kh2, v2, *, nb, n, m1, tok0, k1_tok0, tq):
    two = kh2 is not None
    nq = n // tq
    qb0 = tok0 // tq
    in_specs = [pl.BlockSpec((2, tq, HEAD_PAD), lambda b, p, i: (p, qb0 + b * nq + i, 0)),
                pl.BlockSpec((2, m1, HEAD_PAD), lambda b, p, i: (p, k1_tok0 // m1 + b, 0)),
                pl.BlockSpec((1, m1, LANE), lambda b, p, i: (p, k1_tok0 // m1 + b, 0))]
    args = [qh, kh1, v1]
    if two:
        in_specs += [pl.BlockSpec((2, n, HEAD_PAD), lambda b, p, i: (p, tok0 // n + b, 0)),
                     pl.BlockSpec((1, n, LANE), lambda b, p, i: (p, tok0 // n + b, 0))]
        args += [kh2, v2]
    return pl.pallas_call(
        functools.partial(_attn_body, two=two),
        grid=(nb, H_MLA // 2, nq),
        in_specs=in_specs,
        out_specs=pl.BlockSpec((tq, LANE), lambda b, p, i: (b * nq + i, p)),
        out_shape=jax.ShapeDtypeStruct((nb * n, D_MLA), F32),
        compiler_params=_params("parallel", "parallel", "parallel"),
        name="mla_attention",
    )(*args)


def _rwkv_prep_body(x_ref, xp_ref, xn_ref, mu_ref, vec_ref, up_ref, gup_ref,
                    r_ref, kk_ref, v_ref, lw_ref, kka_ref, kd_ref, bonus_ref, gate_ref,
                    *, tm, ctx_tiles, ctx_tiles_per_seq, lat_tiles_per_seq):
    i = pl.program_id(0)
    in_ctx = i < ctx_tiles
    per_seq = jnp.where(in_ctx, ctx_tiles_per_seq, lat_tiles_per_seq)
    j = jnp.where(in_ctx, i % ctx_tiles_per_seq, jnp.maximum(i - ctx_tiles, 0) % lat_tiles_per_seq)
    first = j == 0
    last = j == per_seq - 1
    x = x_ref[...]
    rowi = lax.broadcasted_iota(jnp.int32, x.shape, 0)
    prev_row = jnp.where(first, 0.0, xp_ref[7:8, :])
    next_row = jnp.where(last, 0.0, xn_ref[0:1, :])
    xprev = jnp.where(rowi == 0, prev_row, pltpu.roll(x, 1, 0))
    xnext = jnp.where(rowi == tm - 1, next_row, pltpu.roll(x, tm - 1, 0))
    xr = x + mu_ref[...] * (0.5 * (xprev + xnext) - x)

    vec = vec_ref[...]
    r = xr[:, 0:D_RWKV]
    kr = xr[:, D_RWKV:2 * D_RWKV]
    vr = xr[:, 2 * D_RWKV:3 * D_RWKV]
    o = 3 * D_RWKV
    zw = (xr[:, o:o + LANE], xr[:, o + LANE:o + 2 * LANE])
    za = (xr[:, o + 2 * LANE:o + 3 * LANE], xr[:, o + 3 * LANE:o + 4 * LANE])
    zg = xr[:, o + 4 * LANE:o + 5 * LANE]

    mm = _head_mean_matrix() * float(RWKV_HD)
    kk = kr * vec[4:5, :]
    kk = kk / (jnp.sqrt(_fdot(kk * kk, mm)) + 1e-12)
    r_ref[...] = r
    kk_ref[...] = kk
    v_ref[...] = vr
    gate_ref[...] = _bdot(_sigmoid(zg), gup_ref[...])
    bonus = jnp.zeros_like(r)
    for z in range(2):
        zwz = _bdot(jnp.tanh(zw[z]), up_ref[z]) + vec[z:z + 1, :]
        lw_ref[z] = -DECAY_SCALE * _sigmoid(zwz)
        a = _sigmoid(_bdot(za[z], up_ref[2 + z]) + vec[2 + z:3 + z, :])
        kd = kr * (1.0 + (a - 1.0) * vec[5:6, :])
        kka_ref[z] = kk * a
        kd_ref[z] = kd
        bonus = bonus + _fdot(r * kd * vec[6:7, :], mm) * vr
    bonus_ref[...] = bonus


def _rwkv_prep(rw, mu, vec, up, gup, *, tm, ctx_tiles, ctx_tiles_per_seq, lat_tiles_per_seq):
    t = rw.shape[0]
    hb = tm // 8
    nhb = t // 8
    row = lambda: pl.BlockSpec((tm, D_RWKV), lambda i: (i, 0))
    two = lambda: pl.BlockSpec((2, tm, D_RWKV), lambda i: (0, i, 0))
    full = lambda a: pl.BlockSpec(a.shape, lambda i: (0,) * a.ndim)
    body = functools.partial(_rwkv_prep_body, tm=tm, ctx_tiles=ctx_tiles, ctx_tiles_per_seq=ctx_tiles_per_seq,
                             lat_tiles_per_seq=lat_tiles_per_seq)
    one = jax.ShapeDtypeStruct((t, D_RWKV), F32)
    both = jax.ShapeDtypeStruct((2, t, D_RWKV), F32)
    return pl.pallas_call(
        body,
        grid=(t // tm,),
        in_specs=[pl.BlockSpec((tm, RW_W), lambda i: (i, 0)),
                  pl.BlockSpec((8, RW_W), lambda i: (jnp.maximum(i * hb - 1, 0), 0)),
                  pl.BlockSpec((8, RW_W), lambda i: (jnp.minimum((i + 1) * hb, nhb - 1), 0)),
                  full(mu), full(vec), full(up), full(gup)],
        out_specs=[row(), row(), row(), two(), two(), two(), row(), row()],
        out_shape=[one, one, one, both, both, both, one, one],
        compiler_params=_params("parallel"),
        name="rwkv_prep",
    )(rw, rw, rw, mu, vec, up, gup)


def _unit_lower_inverse(lmat, rowi, coli):
    eye = jnp.where(rowi == coli, 1.0, 0.0).astype(F32)
    n1 = jnp.where((rowi >> 3) == (coli >> 3), lmat, 0.0)
    n2 = _fdot(n1, n1)
    n4 = _fdot(n2, n2)
    t = _fdot(_fdot(eye - n1, eye + n2), eye + n4)
    for sh in (4, 5, 6):
        inner = (rowi >> (sh - 1)) == (coli >> (sh - 1))
        outer = (rowi >> sh) == (coli >> sh)
        e = jnp.where(jnp.logical_and(outer, jnp.logical_not(inner)), lmat, 0.0)
        t = t - _fdot(t, _fdot(e, t))
    return t


def _rwkv_scan_body(r_ref, kk_ref, v_ref, lw_ref, kka_ref, kd_ref, s0_ref, gn_ref, y_ref, sfin_ref, s_ref):
    z = pl.program_id(1)
    c = pl.program_id(2)

    @pl.when(c == 0)
    def _():
        s_ref[...] = s0_ref[0, 0]

    sgn = 1 - 2 * z
    rowi = lax.broadcasted_iota(jnp.int32, (STACK, STACK), 0)
    coli = lax.broadcasted_iota(jnp.int32, (STACK, STACK), 1)
    dlt = ((coli & (CHUNK - 1)) - (rowi & (CHUNK - 1))) * sgn
    strict = dlt < 0
    incl = dlt <= 0
    ri = lax.broadcasted_iota(jnp.int32, (CHUNK, CHUNK), 0)
    ci = lax.broadcasted_iota(jnp.int32, (CHUNK, CHUNK), 1)
    cmat = jnp.where((ci - ri) * sgn <= 0, 1.0, 0.0).astype(F32)

    lw = lw_ref[0]
    cum = _fdot(cmat, lw)
    wt = jnp.exp(cum)
    iw = jnp.exp(-cum)
    a = jnp.exp(cum - lw) * kk_ref[...]
    b = kka_ref[0] * iw
    cm = kd_ref[0] * iw
    q = r_ref[...] * wt
    v = v_ref[...]
    wtot = jnp.exp(jnp.sum(lw, axis=0, keepdims=True))

    a_s = _stack_heads(a)
    b_s = _stack_heads(b)
    c_s = _stack_heads(cm)
    q_s = _stack_heads(q)
    v_s = _stack_heads(v)
    s = s_ref[...]

    lab = jnp.where(strict, _fdot_nt(a_s, b_s), 0.0)
    lac = jnp.where(strict, _fdot_nt(a_s, c_s), 0.0)
    mqb = jnp.where(incl, _fdot_nt(q_s, b_s), 0.0)
    mqc = jnp.where(incl, _fdot_nt(q_s, c_s), 0.0)
    tinv = _unit_lower_inverse(lab, rowi, coli)
    u_s = -_fdot(tinv, _fdot_nt(a_s, s) + _fdot(lac, v_s))
    y_s = _fdot_nt(q_s, s) + _fdot(mqb, u_s) + _fdot(mqc, v_s)
    s_new = (s + _fdot_tn(u_s, b_s) + _fdot_tn(v_s, c_s)) * wtot
    s_ref[...] = s_new

    y_ref[0] = _head_groupnorm(_unstack_heads(y_s), gn_ref[...], RWKV_GN_EPS)

    @pl.when(c == pl.num_programs(2) - 1)
    def _():
        sfin_ref[0, 0] = s_new


def _rwkv_scan(r, kk, v, lw, kka, kd, s0, gn, *, nb, n, tok0):
    nc = n // CHUNK
    cb0 = tok0 // CHUNK

    def cidx(b, z, c):
        return cb0 + b * nc + c + z * (nc - 1 - 2 * c)

    one = lambda: pl.BlockSpec((CHUNK, D_RWKV), lambda b, z, c: (cidx(b, z, c), 0))
    per_dir = lambda: pl.BlockSpec((1, CHUNK, D_RWKV), lambda b, z, c: (z, cidx(b, z, c), 0))
    state = lambda: pl.BlockSpec((1, 1, STACK, STACK), lambda b, z, c: (b, z, 0, 0))
    return pl.pallas_call(
        _rwkv_scan_body,
        grid=(nb, 2, nc),
        in_specs=[one(), one(), one(), per_dir(), per_dir(), per_dir(), state(),
                  pl.BlockSpec((1, D_RWKV), lambda b, z, c: (0, 0))],
        out_specs=[pl.BlockSpec((1, CHUNK, D_RWKV), lambda b, z, c: (z, b * nc + c + z * (nc - 1 - 2 * c), 0)),
                   state()],
        out_shape=[jax.ShapeDtypeStruct((2, nb * n, D_RWKV), F32),
                   jax.ShapeDtypeStruct((nb, 2, STACK, STACK), F32)],
        scratch_shapes=[pltpu.VMEM((STACK, STACK), F32)],
        compiler_params=_params("parallel", "parallel", "arbitrary"),
        name="rwkv_scan",
    )(r, kk, v, lw, kka, kd, s0, gn)


def _ret_body(q_ref, k_ref, v_ref, g_ref, qs_ref, ks_ref, cr_ref, sr_ref, lgm_ref, lgl_ref, gn_ref, s0_ref,
              o_ref, sfin_ref, s_ref):
    z = pl.program_id(1)
    c = pl.program_id(2)

    @pl.when(c == 0)
    def _():
        s_ref[...] = s0_ref[0, 0]

    sgn = 1 - 2 * z
    cr = cr_ref[...]
    sr = sr_ref[...]
    q = q_ref[...] * cr + qs_ref[...] * sr
    k = (k_ref[...] * cr + ks_ref[...] * sr) * (RET_DK ** -0.5)
    v = v_ref[...]
    lgl = lgl_ref[0]
    ti = lax.broadcasted_iota(jnp.int32, (CHUNK, D_RET), 0)
    pos = jnp.where(z == 0, ti, CHUNK - 1 - ti).astype(F32)
    xi = jnp.exp(lgl * (pos + 1.0))
    zeta = jnp.exp(lgl * (CHUNK - 1.0 - pos))

    rowi = lax.broadcasted_iota(jnp.int32, (STACK, STACK), 0)
    coli = lax.broadcasted_iota(jnp.int32, (STACK, STACK), 1)
    dist = ((rowi & (CHUNK - 1)) - (coli & (CHUNK - 1))) * sgn
    dmask = jnp.where(dist >= 0, jnp.exp(lgm_ref[0] * jnp.maximum(dist, 0).astype(F32)), 0.0)

    q_s = _stack_heads(q)
    k_s = _stack_heads(k)
    v_s = _stack_heads(v)
    s = s_ref[...]
    scores = _bdot_nt(q_s, k_s) * dmask
    o = _unstack_heads(_bdot(scores, v_s)) + _bdot(q * xi, s)
    s_new = s * jnp.exp(lgl * float(CHUNK)) + _bdot_tn(_stack_heads(k * zeta), v_s)
    s_ref[...] = s_new

    g = g_ref[...]
    o_ref[0] = _head_groupnorm(o, gn_ref[0], RET_GN_EPS) * (g * _sigmoid(g))

    @pl.when(c == pl.num_programs(2) - 1)
    def _():
        sfin_ref[0, 0] = s_new


def _retention(rt, rts, cr, sr, lgm, lgl, gn, s0, *, nb, n, tok0):
    nc = n // CHUNK
    cb0 = tok0 // CHUNK

    def cidx(b, z, c):
        return cb0 + b * nc + c + z * (nc - 1 - 2 * c)

    col = lambda j: pl.BlockSpec((CHUNK, D_RET), lambda b, z, c: (cidx(b, z, c), j))
    state = lambda: pl.BlockSpec((1, 1, STACK, STACK), lambda b, z, c: (b, z, 0, 0))
    return pl.pallas_call(
        _ret_body,
        grid=(nb, 2, nc),
        in_specs=[col(0), col(1), col(2),
                  pl.BlockSpec((CHUNK, D_RET), lambda b, z, c: (cidx(b, z, c), 3 + z)),
                  col(0), col(1),
                  pl.BlockSpec((CHUNK, D_RET), lambda b, z, c: (cidx(b, z, c), 0)),
                  pl.BlockSpec((CHUNK, D_RET), lambda b, z, c: (cidx(b, z, c), 0)),
                  pl.BlockSpec((1, STACK, STACK), lambda b, z, c: (z, 0, 0)),
                  pl.BlockSpec((1, 1, D_RET), lambda b, z, c: (z, 0, 0)),
                  pl.BlockSpec((1, 1, D_RET), lambda b, z, c: (z, 0, 0)),
                  state()],
        out_specs=[pl.BlockSpec((1, CHUNK, D_RET), lambda b, z, c: (z, b * nc + c + z * (nc - 1 - 2 * c), 0)),
                   state()],
        out_shape=[jax.ShapeDtypeStruct((2, nb * n, D_RET), F32),
                   jax.ShapeDtypeStruct((nb, 2, STACK, STACK), F32)],
        scratch_shapes=[pltpu.VMEM((STACK, STACK), F32)],
        compiler_params=_params("parallel", "parallel", "arbitrary"),
        name="retention",
    )(rt, rt, rt, rt, rts, rts, cr, sr, lgm, lgl, gn, s0)


def _mixout_body(x_ref, mod_ref, ng_ref, mla_ref, yr_ref, bonus_ref, gate_ref, ret_ref, w_ref, o_ref):
    m = mod_ref[0]
    rwkv_o = (yr_ref[0] + yr_ref[1] + bonus_ref[...]) * gate_ref[...]
    ret_o = ret_ref[0] + ret_ref[1]
    mixed = (_bdot(mla_ref[...], w_ref[0:D_MLA, :])
             + _bdot(rwkv_o, w_ref[D_MLA:D_MLA + D_RWKV, :])
             + _bdot(ret_o, w_ref[D_MLA + D_RWKV:, :]))
    o_ref[...] = x_ref[...] + m[5:6, :] * _rms(mixed, ng_ref[3:4, :])


def _mixout(x, mod, ng, mla_o, yr, bonus, gate, ret, w_out, grp, tm):
    t = x.shape[0]
    row = lambda w: pl.BlockSpec((tm, w), lambda i: (i, 0))
    two = lambda w: pl.BlockSpec((2, tm, w), lambda i: (0, i, 0))
    return pl.pallas_call(
        _mixout_body,
        grid=(t // tm,),
        in_specs=[row(D_MODEL),
                  pl.BlockSpec((1, N_MOD, D_MODEL), lambda i: (grp(i), 0, 0)),
                  pl.BlockSpec((6, D_MODEL), lambda i: (0, 0)),
                  row(D_MLA), two(D_RWKV), row(D_RWKV), row(D_RWKV), two(D_RET),
                  pl.BlockSpec(w_out.shape, lambda i: (0, 0))],
        out_specs=row(D_MODEL),
        out_shape=jax.ShapeDtypeStruct((t, D_MODEL), F32),
        compiler_params=_params("parallel"),
        name="mixer_out",
    )(x, mod, ng, mla_o, yr, bonus, gate, ret, w_out)


def _rope_perm(d):
    q = d // 4
    base = np.concatenate([np.arange(q, 2 * q), np.arange(0, q)])
    return np.concatenate([base, base + 2 * q])


def _rope_tables(row, col, d):
    half = d // 4
    inv = ROPE_BASE ** (-jnp.arange(half, dtype=F32) / half)
    parts_c, parts_s = [], []
    for pos in (row, col):
        ang = pos.astype(F32)[:, None] * inv
        cos, sin = jnp.cos(ang), jnp.sin(ang)
        parts_c += [cos, cos]
        parts_s += [-sin, sin]
    return jnp.concatenate(parts_c, axis=-1), jnp.concatenate(parts_s, axis=-1)


def _layout_w_in(w_in):
    depth = w_in.shape[0]
    zeros = lambda w: jnp.zeros((depth, D_MODEL, w), w_in.dtype)
    o = 0
    qa = w_in[..., o:o + Q_LORA]; o += Q_LORA
    kva = w_in[..., o:o + KV_LORA]; o += KV_LORA
    kr = w_in[..., o:o + ROPE_DIM]; o += ROPE_DIM
    rw = w_in[..., o:o + 3 * D_RWKV + 4 * LORA_WA + LORA_G]; o += 3 * D_RWKV + 4 * LORA_WA + LORA_G
    rt = w_in[..., o:]
    kr_blk = lambda x: jnp.concatenate([zeros(ROPE_OFF), x, zeros(HEAD_PAD - ROPE_OFF - ROPE_DIM)], axis=-1)
    rw_parts = [rw[..., :3 * D_RWKV]]
    for j in range(4):
        lo = 3 * D_RWKV + j * LORA_WA
        rw_parts += [rw[..., lo:lo + LORA_WA], zeros(LANE - LORA_WA)]
    rw_parts.append(rw[..., 3 * D_RWKV + 4 * LORA_WA:])
    perm = np.concatenate([h * RET_DK + _rope_perm(RET_DK) for h in range(2 * H_RET)])
    cols = [qa, kva, kr_blk(kr), kr_blk(kr[..., _rope_perm(ROPE_DIM)])] + rw_parts + [rt, rt[..., perm]]
    return jnp.concatenate(cols, axis=-1).astype(BF16)


def _layout_mu(mu):
    depth = mu.shape[0]
    parts = [mu[:, :3 * D_RWKV]]
    for j in range(4):
        lo = 3 * D_RWKV + j * LORA_WA
        parts += [mu[:, lo:lo + LORA_WA], jnp.zeros((depth, LANE - LORA_WA), mu.dtype)]
    parts.append(mu[:, 3 * D_RWKV + 4 * LORA_WA:])
    return jnp.concatenate(parts, axis=-1).reshape(depth, 1, RW_W)


def _layout_wq(wq):
    depth = wq.shape[0]
    w = wq.reshape(depth, Q_LORA, H_MLA, QK_NOPE + ROPE_DIM)
    pad = jnp.zeros((depth, Q_LORA, H_MLA, HEAD_PAD - QK_NOPE - ROPE_DIM), wq.dtype)
    nope, rope = w[..., :QK_NOPE], w[..., QK_NOPE:]
    main = jnp.concatenate([nope, rope, pad], axis=-1)
    swap = jnp.concatenate([jnp.zeros_like(nope), rope[..., _rope_perm(ROPE_DIM)], pad], axis=-1)
    shape = (depth, Q_LORA, H_MLA * HEAD_PAD)
    return main.reshape(shape).astype(BF16), swap.reshape(shape).astype(BF16)


def _layout_wkv(wkv):
    depth = wkv.shape[0]
    w = wkv.reshape(depth, KV_LORA, H_MLA, QK_NOPE + V_HD)
    kn = jnp.concatenate([w[..., :QK_NOPE], jnp.zeros((depth, KV_LORA, H_MLA, HEAD_PAD - QK_NOPE), wkv.dtype)], axis=-1)
    return (kn.reshape(depth, KV_LORA, H_MLA * HEAD_PAD).astype(BF16),
            w[..., QK_NOPE:].reshape(depth, KV_LORA, D_MLA).astype(BF16))


def _layout_lora_up(up):
    depth = up.shape[0]
    return jnp.concatenate([up, jnp.zeros((depth, 4, LANE - LORA_WA, D_RWKV), up.dtype)], axis=2).astype(BF16)


def _block_diag(s):
    eye = jnp.eye(H_RWKV, dtype=s.dtype)
    out = jnp.einsum('...hij,hg->...higj', s, eye)
    return out.reshape(s.shape[:-3] + (STACK, STACK))


def _diag_blocks(s):
    s5 = s.reshape(s.shape[:-2] + (H_RWKV, RWKV_HD, H_RWKV, RWKV_HD))
    return jnp.stack([s5[..., h, :, h, :] for h in range(H_RWKV)], axis=-3)


def kernel(x_prompt, x_sample, cache_mla_ckv, cache_mla_krope, state_rwkv, state_ret, c, c_ctx,
           norm_g, w_mod, b_mod, ffn_wi, ffn_wo, w_in, w_out, mla_norm_q, mla_norm_kv, mla_wq_up,
           mla_wkv_up, rwkv_mu, rwkv_vec, rwkv_lora_up, rwkv_g_up, ret_gn):
    bc, nc, _ = x_prompt.shape
    bl, nl, _ = x_sample.shape
    depth = norm_g.shape[0]
    past = cache_mla_ckv.shape[2]
    tc, tl = bc * nc, bl * nl
    t_all = tc + tl
    assert bl <= 7 and nl % GRID_W == 0
    assert nc % CHUNK == 0 and nl % CHUNK == 0 and tc % nl == 0 and tc % past == 0

    def tile(cap):
        tm = cap
        while tc % tm or nl % tm:
            tm //= 2
        return tm

    def grouper(tm):
        ctx_tiles, per_seq = tc // tm, nl // tm
        return lambda i: jnp.where(i < ctx_tiles, 0, 1 + jnp.maximum(i - ctx_tiles, 0) // per_seq)

    tm_ffn, tm_tok = tile(1024), tile(512)
    tm_shift = 256
    while nc % tm_shift or nl % tm_shift:
        tm_shift //= 2
    tq_ctx, tq_lat = min(nc, 256), min(nl, 256)

    wi_b, wo_b = ffn_wi.astype(BF16), ffn_wo.astype(BF16)
    w_in_b = _layout_w_in(w_in)
    w_out_b = w_out.astype(BF16)
    wq_b, wqs_b = _layout_wq(mla_wq_up)
    wk_b, wv_b = _layout_wkv(mla_wkv_up)
    mu_l = _layout_mu(rwkv_mu)
    up_b = _layout_lora_up(rwkv_lora_up)
    gup_b = rwkv_g_up.astype(BF16)

    pos = jnp.arange(nl)
    row_l, col_l = pos // GRID_W, pos % GRID_W
    cos_m, sin_m = _rope_tables(row_l, col_l, ROPE_DIM)
    cos_r, sin_r = _rope_tables(row_l, col_l, RET_DK)

    def mla_table(tab, fill):
        lat = jnp.concatenate([jnp.full((nl, ROPE_OFF), fill, F32), tab,
                               jnp.zeros((nl, HEAD_PAD - ROPE_OFF - ROPE_DIM), F32)], axis=-1)
        ctx = jnp.concatenate([jnp.full((tc, ROPE_OFF + ROPE_DIM), fill, F32),
                               jnp.zeros((tc, HEAD_PAD - ROPE_OFF - ROPE_DIM), F32)], axis=-1)
        return jnp.concatenate([ctx, jnp.tile(lat, (bl, 1))], axis=0)

    def ret_table(tab, fill):
        return jnp.concatenate([jnp.full((tc, D_RET), fill, F32), jnp.tile(jnp.tile(tab, (1, H_RET)), (bl, 1))], axis=0)

    cq, sq = mla_table(cos_m, 1.0), mla_table(sin_m, 0.0)
    cr, sr = ret_table(cos_r, 1.0), ret_table(sin_r, 0.0)

    e = 5.0 + jnp.arange(H_RET, dtype=F32)[None, :] + 0.5 * jnp.arange(2, dtype=F32)[:, None]
    lg = jnp.log1p(-jnp.exp2(-e))
    lg_lane = jnp.repeat(lg, RET_DK, axis=1)
    lgm = jnp.broadcast_to(lg_lane[:, :, None], (2, STACK, STACK))
    lgl = lg_lane[:, None, :]

    c_all = jnp.concatenate([c_ctx[None, :], c, jnp.zeros((7 - bl, D_MODEL), F32)], axis=0)
    mod = _modulation(c_all, w_mod, b_mod).reshape(depth, 8, N_MOD, D_MODEL)

    kr_cache = jnp.pad(cache_mla_krope, ((0, 0), (0, 0), (0, 0), (ROPE_OFF, HEAD_PAD - ROPE_OFF - ROPE_DIM)))
    zero_state = jnp.zeros((bc, 2, STACK, STACK), F32)
    s0_rwkv_lat = _block_diag(state_rwkv.astype(F32))
    s0_ret_lat = _block_diag(state_ret.astype(F32))

    x = jnp.concatenate([x_prompt.reshape(tc, D_MODEL), x_sample.reshape(tl, D_MODEL)], axis=0)
    ckv_l, krope_l, rwkv_l, ret_l = [], [], [], []
    for l in range(depth):
        mod_l, ng = mod[l], norm_g[l]
        x = _ffn(x, mod_l, ng, wi_b, wo_b, l, 0, grouper(tm_ffn), tm_ffn)
        qa, kva, kr, krs, rw, rt, rts = _mixin(x, mod_l, ng, w_in_b[l], grouper(tm_tok), tm_tok)

        qh, ckv, kh, vv = _mla_prep(qa, kva, kr, krs, cq, sq, mla_norm_q[l][None, :], mla_norm_kv[l][None, :],
                                    wq_b[l], wqs_b[l], wk_b[l], wv_b[l], tm_tok)
        kh_c, v_c = _cache_kv(cache_mla_ckv[:, l].reshape(bl * past, KV_LORA),
                              kr_cache[:, l].reshape(bl * past, HEAD_PAD), wk_b[l], wv_b[l])
        mla_ctx = _attention(qh, kh, vv, None, None, nb=bc, n=nc, m1=nc, tok0=0, k1_tok0=0, tq=tq_ctx)
        mla_lat = _attention(qh, kh_c, v_c, kh, vv, nb=bl, n=nl, m1=past, tok0=tc, k1_tok0=0, tq=tq_lat)
        mla_o = jnp.concatenate([mla_ctx, mla_lat], axis=0)

        r, kk, v, lw, kka, kd, bonus, gate = _rwkv_prep(
            rw, mu_l[l], rwkv_vec[l], up_b[l], gup_b[l],
            tm=tm_shift, ctx_tiles=tc // tm_shift, ctx_tiles_per_seq=nc // tm_shift,
            lat_tiles_per_seq=nl // tm_shift)
        gn_r = rwkv_vec[l][7:8, :]
        y_ctx, s_rwkv = _rwkv_scan(r, kk, v, lw, kka, kd, zero_state, gn_r, nb=bc, n=nc, tok0=0)
        y_lat, _ = _rwkv_scan(r, kk, v, lw, kka, kd, s0_rwkv_lat[:, l], gn_r, nb=bl, n=nl, tok0=tc)
        yr = jnp.concatenate([y_ctx, y_lat], axis=1)

        gn_t = ret_gn[l][:, None, :]
        o_ctx, s_ret = _retention(rt, rts, cr, sr, lgm, lgl, gn_t, zero_state, nb=bc, n=nc, tok0=0)
        o_lat, _ = _retention(rt, rts, cr, sr, lgm, lgl, gn_t, s0_ret_lat[:, l], nb=bl, n=nl, tok0=tc)
        ret = jnp.concatenate([o_ctx, o_lat], axis=1)

        x = _mixout(x, mod_l, ng, mla_o, yr, bonus, gate, ret, w_out_b[l], grouper(tm_tok), tm_tok)
        x = _ffn(x, mod_l, ng, wi_b, wo_b, l, 1, grouper(tm_ffn), tm_ffn)

        ckv_l.append(ckv[:tc].reshape(bc, nc, KV_LORA))
        krope_l.append(kr[:tc, ROPE_OFF:ROPE_OFF + ROPE_DIM].reshape(bc, nc, ROPE_DIM))
        rwkv_l.append(_diag_blocks(s_rwkv))
        ret_l.append(_diag_blocks(s_ret))

    return (x[:tc].reshape(bc, nc, D_MODEL), x[tc:].reshape(bl, nl, D_MODEL),
            jnp.stack(ckv_l, axis=1), jnp.stack(krope_l, axis=1),
            jnp.stack(rwkv_l, axis=1), jnp.stack(ret_l, axis=1))
```

```python
import functools

import numpy as np
import jax
import jax.numpy as jnp
from jax import lax
from jax.experimental import pallas as pl
from jax.experimental.pallas import tpu as pltpu

F32 = jnp.float32
BF16 = jnp.bfloat16
HIGHEST = lax.Precision.HIGHEST

D_MODEL = 1024
N_MOD = 9
D_FF = 2816
H_MLA = 8
QK_NOPE = 64
ROPE_DIM = 32
V_HD = 64
Q_LORA = 256
KV_LORA = 128
D_MLA = H_MLA * V_HD
MLA_SCALE = (QK_NOPE + ROPE_DIM) ** -0.5
H_RWKV = 4
RWKV_HD = 64
D_RWKV = H_RWKV * RWKV_HD
LORA_WA = 64
LORA_G = 128
DECAY_SCALE = 0.6065306597126334
RWKV_GN_EPS = 64e-5
H_RET = 4
RET_DK = 64
D_RET = H_RET * 64
RET_GN_EPS = 1e-5
ROPE_BASE = 10000.0
NORM_EPS = 1e-6
GRID_W = 64

LANE = 128
HEAD_PAD = 128
ROPE_OFF = QK_NOPE
CHUNK = 64
STACK = H_RWKV * CHUNK
VMEM_LIMIT = 56 * 1024 * 1024

C_QA = 0
C_KVA = C_QA + Q_LORA
C_KR = C_KVA + KV_LORA
C_KRS = C_KR + LANE
C_RW = C_KRS + LANE
RW_W = 3 * D_RWKV + 4 * LANE + LORA_G
C_RT = C_RW + RW_W
RT_W = 2 * H_RET * RET_DK + 3 * D_RET
C_RTS = C_RT + RT_W
RTS_W = 2 * H_RET * RET_DK
IN_W = C_RTS + RTS_W


def _params(*sem):
    return pltpu.CompilerParams(dimension_semantics=sem, vmem_limit_bytes=VMEM_LIMIT)


def _sigmoid(x):
    return 1.0 / (1.0 + jnp.exp(-x))


def _rms(x, g):
    return x * lax.rsqrt(jnp.mean(x * x, axis=-1, keepdims=True) + NORM_EPS) * g


def _bdot(a, b):
    return jnp.dot(a.astype(BF16), b.astype(BF16), preferred_element_type=F32)


def _bdot_nt(a, b):
    return lax.dot_general(a.astype(BF16), b.astype(BF16), (((1,), (1,)), ((), ())),
                           preferred_element_type=F32)


def _split_bf16(x):
    hi = x.astype(BF16)
    return hi, (x - hi.astype(F32)).astype(BF16)


def _dot_exact_rhs(x, m):
    hi, lo = _split_bf16(x)
    mb = m.astype(BF16)
    return jnp.dot(hi, mb, preferred_element_type=F32) + jnp.dot(lo, mb, preferred_element_type=F32)


def _dot_exact_lhs(m, x):
    hi, lo = _split_bf16(x)
    mb = m.astype(BF16)
    return jnp.dot(mb, hi, preferred_element_type=F32) + jnp.dot(mb, lo, preferred_element_type=F32)


def _bdot_tn(a, b):
    return lax.dot_general(a.astype(BF16), b.astype(BF16), (((0,), (0,)), ((), ())),
                           preferred_element_type=F32)


def _head_mean_matrix():
    r = lax.broadcasted_iota(jnp.int32, (STACK, STACK), 0) // RWKV_HD
    c = lax.broadcasted_iota(jnp.int32, (STACK, STACK), 1) // RWKV_HD
    return jnp.where(r == c, 1.0 / RWKV_HD, 0.0).astype(F32)


def _head_groupnorm(y, gain, eps):
    mm = _head_mean_matrix()
    yc = y - _dot_exact_rhs(y, mm)
    var = _dot_exact_rhs(yc * yc, mm)
    return yc * lax.rsqrt(var + eps) * gain


def _stack_heads(x):
    head = lax.broadcasted_iota(jnp.int32, x.shape, 1) // RWKV_HD
    return jnp.concatenate([jnp.where(head == h, x, 0.0) for h in range(H_RWKV)], axis=0)


def _unstack_heads(x):
    out = x[0:CHUNK]
    for h in range(1, H_RWKV):
        out = out + x[h * CHUNK:(h + 1) * CHUNK]
    return out


def _mod_body(c_ref, w_ref, b_ref, o_ref):
    c = c_ref[...]
    s = c * _sigmoid(c)
    o_ref[0] = _bdot(s, w_ref[0]) + b_ref[0]


def _modulation(c_all, w_mod, b_mod):
    depth, d, nd = w_mod.shape
    tn = D_MODEL
    return pl.pallas_call(
        _mod_body,
        grid=(depth, nd // tn),
        in_specs=[pl.BlockSpec((8, d), lambda l, j: (0, 0)),
                  pl.BlockSpec((1, d, tn), lambda l, j: (l, 0, j)),
                  pl.BlockSpec((1, 1, tn), lambda l, j: (l, 0, j))],
        out_specs=pl.BlockSpec((1, 8, tn), lambda l, j: (l, 0, j)),
        out_shape=jax.ShapeDtypeStruct((depth, 8, nd), F32),
        compiler_params=_params("parallel", "parallel"),
        name="modulation",
    )(c_all, w_mod, b_mod.reshape(depth, 1, nd))


def _ffn_body(x_ref, mod_ref, ng_ref, wg_ref, wu_ref, wo_ref, o_ref, h_ref, acc_ref, *, mrow, grow):
    f = pl.program_id(1)

    @pl.when(f == 0)
    def _():
        m = mod_ref[0]
        h = _rms(x_ref[...], ng_ref[grow:grow + 1, :]) * (1.0 + m[mrow + 1:mrow + 2, :]) + m[mrow:mrow + 1, :]
        h_ref[...] = h.astype(BF16)
        acc_ref[...] = jnp.zeros_like(acc_ref)

    h = h_ref[...]
    g = jnp.dot(h, wg_ref[...], preferred_element_type=F32)
    u = jnp.dot(h, wu_ref[...], preferred_element_type=F32)
    a = (g * _sigmoid(g)) * u
    acc_ref[...] += jnp.dot(a.astype(BF16), wo_ref[...], preferred_element_type=F32)

    @pl.when(f == pl.num_programs(1) - 1)
    def _():
        m = mod_ref[0]
        o_ref[...] = x_ref[...] + 0.5 * m[mrow + 2:mrow + 3, :] * _rms(acc_ref[...], ng_ref[grow + 1:grow + 2, :])


def _ffn(x, mod, ng, wi, wo, layer, which, grp, tm):
    t = x.shape[0]
    tf = 256
    nf = D_FF // tf
    body = functools.partial(_ffn_body, mrow=6 * which, grow=4 * which)
    return pl.pallas_call(
        body,
        grid=(t // tm, nf),
        in_specs=[pl.BlockSpec((tm, D_MODEL), lambda i, f: (i, 0)),
                  pl.BlockSpec((1, N_MOD, D_MODEL), lambda i, f: (grp(i), 0, 0)),
                  pl.BlockSpec((6, D_MODEL), lambda i, f: (0, 0)),
                  pl.BlockSpec((None, None, D_MODEL, tf), lambda i, f: (layer, which, 0, f)),
                  pl.BlockSpec((None, None, D_MODEL, tf), lambda i, f: (layer, which, 0, f + nf)),
                  pl.BlockSpec((None, None, tf, D_MODEL), lambda i, f: (layer, which, f, 0))],
        out_specs=pl.BlockSpec((tm, D_MODEL), lambda i, f: (i, 0)),
        out_shape=jax.ShapeDtypeStruct((t, D_MODEL), F32),
        scratch_shapes=[pltpu.VMEM((tm, D_MODEL), BF16), pltpu.VMEM((tm, D_MODEL), F32)],
        compiler_params=_params("parallel", "arbitrary"),
        name="ffn",
    )(x, mod, ng, wi, wi, wo)


def _mixin_body(x_ref, mod_ref, ng_ref, w_ref, qa_ref, kva_ref, kr_ref, krs_ref, rw_ref, rt_ref, rts_ref):
    m = mod_ref[0]
    h = (_rms(x_ref[...], ng_ref[2:3, :]) * (1.0 + m[4:5, :]) + m[3:4, :]).astype(BF16)

    def proj(lo, width):
        return jnp.dot(h, w_ref[:, lo:lo + width], preferred_element_type=F32)

    qa_ref[...] = proj(C_QA, Q_LORA)
    kva_ref[...] = proj(C_KVA, KV_LORA)
    kr_ref[...] = proj(C_KR, LANE)
    krs_ref[...] = proj(C_KRS, LANE)
    rw_ref[...] = proj(C_RW, RW_W)
    rt_ref[...] = proj(C_RT, RT_W)
    rts_ref[...] = proj(C_RTS, RTS_W)


def _mixin(x, mod, ng, w_in, grp, tm):
    t = x.shape[0]
    widths = (Q_LORA, KV_LORA, LANE, LANE, RW_W, RT_W, RTS_W)
    return pl.pallas_call(
        _mixin_body,
        grid=(t // tm,),
        in_specs=[pl.BlockSpec((tm, D_MODEL), lambda i: (i, 0)),
                  pl.BlockSpec((1, N_MOD, D_MODEL), lambda i: (grp(i), 0, 0)),
                  pl.BlockSpec((6, D_MODEL), lambda i: (0, 0)),
                  pl.BlockSpec((D_MODEL, IN_W), lambda i: (0, 0))],
        out_specs=[pl.BlockSpec((tm, w), lambda i: (i, 0)) for w in widths],
        out_shape=[jax.ShapeDtypeStruct((t, w), F32) for w in widths],
        compiler_params=_params("parallel"),
        name="mixer_in",
    )(x, mod, ng, w_in)


def _kv_expand(ckv, krr, wk_ref, wv_ref, kh_ref, v_ref):
    c = ckv.astype(BF16)
    kn = jnp.dot(c, wk_ref[...], preferred_element_type=F32)
    v = jnp.dot(c, wv_ref[...], preferred_element_type=F32)
    for h in range(H_MLA):
        kh_ref[h] = (kn[:, h * HEAD_PAD:(h + 1) * HEAD_PAD] + krr).astype(BF16)
    for p in range(H_MLA // 2):
        v_ref[p] = v[:, p * LANE:(p + 1) * LANE].astype(BF16)


def _mla_prep_body(qa_ref, kva_ref, kr_ref, krs_ref, cq_ref, sq_ref, gq_ref, gkv_ref, wq_ref, wqs_ref,
                   wk_ref, wv_ref, qh_ref, ckv_ref, kh_ref, v_ref):
    cq = cq_ref[...]
    sq = sq_ref[...]
    qn = _rms(qa_ref[...], gq_ref[...]).astype(BF16)
    q = jnp.dot(qn, wq_ref[...], preferred_element_type=F32)
    qs = jnp.dot(qn, wqs_ref[...], preferred_element_type=F32)
    for h in range(H_MLA):
        sl = slice(h * HEAD_PAD, (h + 1) * HEAD_PAD)
        qh_ref[h] = (q[:, sl] * cq + qs[:, sl] * sq).astype(BF16)
    ckv = _rms(kva_ref[...], gkv_ref[...])
    ckv_ref[...] = ckv
    krr = kr_ref[...] * cq + krs_ref[...] * sq
    _kv_expand(ckv, krr, wk_ref, wv_ref, kh_ref, v_ref)


def _mla_prep(qa, kva, kr, krs, cq, sq, gq, gkv, wq, wqs, wk, wv, tm):
    t = qa.shape[0]
    row = lambda w: pl.BlockSpec((tm, w), lambda i: (i, 0))
    full = lambda a: pl.BlockSpec(a.shape, lambda i: (0,) * a.ndim)
    return pl.pallas_call(
        _mla_prep_body,
        grid=(t // tm,),
        in_specs=[row(Q_LORA), row(KV_LORA), row(LANE), row(LANE), row(LANE), row(LANE),
                  full(gq), full(gkv), full(wq), full(wqs), full(wk), full(wv)],
        out_specs=[pl.BlockSpec((H_MLA, tm, HEAD_PAD), lambda i: (0, i, 0)),
                   row(KV_LORA),
                   pl.BlockSpec((H_MLA, tm, HEAD_PAD), lambda i: (0, i, 0)),
                   pl.BlockSpec((H_MLA // 2, tm, LANE), lambda i: (0, i, 0))],
        out_shape=[jax.ShapeDtypeStruct((H_MLA, t, HEAD_PAD), BF16),
                   jax.ShapeDtypeStruct((t, KV_LORA), F32),
                   jax.ShapeDtypeStruct((H_MLA, t, HEAD_PAD), BF16),
                   jax.ShapeDtypeStruct((H_MLA // 2, t, LANE), BF16)],
        compiler_params=_params("parallel"),
        name="mla_prep",
    )(qa, kva, kr, krs, cq, sq, gq, gkv, wq, wqs, wk, wv)


def _cache_kv_body(ckv_ref, kr_ref, wk_ref, wv_ref, kh_ref, v_ref):
    _kv_expand(ckv_ref[...], kr_ref[...], wk_ref, wv_ref, kh_ref, v_ref)


def _cache_kv(ckv, kr, wk, wv):
    t = ckv.shape[0]
    tm = min(t, 512)
    row = lambda w: pl.BlockSpec((tm, w), lambda i: (i, 0))
    full = lambda a: pl.BlockSpec(a.shape, lambda i: (0,) * a.ndim)
    return pl.pallas_call(
        _cache_kv_body,
        grid=(t // tm,),
        in_specs=[row(KV_LORA), row(LANE), full(wk), full(wv)],
        out_specs=[pl.BlockSpec((H_MLA, tm, HEAD_PAD), lambda i: (0, i, 0)),
                   pl.BlockSpec((H_MLA // 2, tm, LANE), lambda i: (0, i, 0))],
        out_shape=[jax.ShapeDtypeStruct((H_MLA, t, HEAD_PAD), BF16),
                   jax.ShapeDtypeStruct((H_MLA // 2, t, LANE), BF16)],
        compiler_params=_params("parallel"),
        name="cache_kv",
    )(ckv, kr, wk, wv)


def _attn_body(*refs, two):
    if two:
        q_ref, k1_ref, v1_ref, k2_ref, v2_ref, o_ref = refs
    else:
        q_ref, k1_ref, v1_ref, o_ref = refs
    outs = []
    for j in range(2):
        q = q_ref[j]
        s1 = lax.dot_general(q, k1_ref[j], (((1,), (1,)), ((), ())), preferred_element_type=F32) * MLA_SCALE
        m = jnp.max(s1, axis=-1, keepdims=True)
        if two:
            s2 = lax.dot_general(q, k2_ref[j], (((1,), (1,)), ((), ())), preferred_element_type=F32) * MLA_SCALE
            m = jnp.maximum(m, jnp.max(s2, axis=-1, keepdims=True))
        p1 = jnp.exp(s1 - m)
        l = jnp.sum(p1, axis=-1, keepdims=True)
        o = jnp.dot(p1.astype(BF16), v1_ref[0], preferred_element_type=F32)
        if two:
            p2 = jnp.exp(s2 - m)
            l = l + jnp.sum(p2, axis=-1, keepdims=True)
            o = o + jnp.dot(p2.astype(BF16), v2_ref[0], preferred_element_type=F32)
        outs.append(o / l)
    lane = lax.broadcasted_iota(jnp.int32, outs[0].shape, 1)
    o_ref[...] = jnp.where(lane < V_HD, outs[0], outs[1])


def _attention(qh, kh1, v1, kh2, v2, *, nb, n, m1, tok0, k1_tok0, tq):
    two = kh2 is not None
    nq = n // tq
    qb0 = tok0 // tq
    in_specs = [pl.BlockSpec((2, tq, HEAD_PAD), lambda b, p, i: (p, qb0 + b * nq + i, 0)),
                pl.BlockSpec((2, m1, HEAD_PAD), lambda b, p, i: (p, k1_tok0 // m1 + b, 0)),
                pl.BlockSpec((1, m1, LANE), lambda b, p, i: (p, k1_tok0 // m1 + b, 0))]
    args = [qh, kh1, v1]
    if two:
        in_specs += [pl.BlockSpec((2, n, HEAD_PAD), lambda b, p, i: (p, tok0 // n + b, 0)),
                     pl.BlockSpec((1, n, LANE), lambda b, p, i: (p, tok0 // n + b, 0))]
        args += [kh2, v2]
    return pl.pallas_call(
        functools.partial(_attn_body, two=two),
        grid=(nb, H_MLA // 2, nq),
        in_specs=in_specs,
        out_specs=pl.BlockSpec((tq, LANE), lambda b, p, i: (b * nq + i, p)),
        out_shape=jax.ShapeDtypeStruct((nb * n, D_MLA), F32),
        compiler_params=_params("parallel", "parallel", "parallel"),
        name="mla_attention",
    )(*args)


def _rwkv_prep_body(x_ref, xp_ref, xn_ref, mu_ref, vec_ref, up_ref, gup_ref,
                    r_ref, kk_ref, v_ref, lw_ref, kka_ref, kd_ref, bonus_ref, gate_ref,
                    *, tm, ctx_tiles, ctx_tiles_per_seq, lat_tiles_per_seq):
    i = pl.program_id(0)
    in_ctx = i < ctx_tiles
    per_seq = jnp.where(in_ctx, ctx_tiles_per_seq, lat_tiles_per_seq)
    j = jnp.where(in_ctx, i % ctx_tiles_per_seq, jnp.maximum(i - ctx_tiles, 0) % lat_tiles_per_seq)
    first = j == 0
    last = j == per_seq - 1
    x = x_ref[...]
    rowi = lax.broadcasted_iota(jnp.int32, x.shape, 0)
    prev_row = jnp.where(first, 0.0, xp_ref[7:8, :])
    next_row = jnp.where(last, 0.0, xn_ref[0:1, :])
    xprev = jnp.where(rowi == 0, prev_row, pltpu.roll(x, 1, 0))
    xnext = jnp.where(rowi == tm - 1, next_row, pltpu.roll(x, tm - 1, 0))
    xr = x + mu_ref[...] * (0.5 * (xprev + xnext) - x)

    vec = vec_ref[...]
    r = xr[:, 0:D_RWKV]
    kr = xr[:, D_RWKV:2 * D_RWKV]
    vr = xr[:, 2 * D_RWKV:3 * D_RWKV]
    o = 3 * D_RWKV
    zw = (xr[:, o:o + LANE], xr[:, o + LANE:o + 2 * LANE])
    za = (xr[:, o + 2 * LANE:o + 3 * LANE], xr[:, o + 3 * LANE:o + 4 * LANE])
    zg = xr[:, o + 4 * LANE:o + 5 * LANE]

    mm = _head_mean_matrix() * float(RWKV_HD)
    kk = kr * vec[4:5, :]
    kk = kk / (jnp.sqrt(_dot_exact_rhs(kk * kk, mm)) + 1e-12)
    r_ref[...] = r
    kk_ref[...] = kk
    v_ref[...] = vr
    gate_ref[...] = _bdot(_sigmoid(zg), gup_ref[...])
    bonus = jnp.zeros_like(r)
    for z in range(2):
        zwz = _bdot(jnp.tanh(zw[z]), up_ref[z]) + vec[z:z + 1, :]
        lw_ref[z] = -DECAY_SCALE * _sigmoid(zwz)
        a = _sigmoid(_bdot(za[z], up_ref[2 + z]) + vec[2 + z:3 + z, :])
        kd = kr * (1.0 + (a - 1.0) * vec[5:6, :])
        kka_ref[z] = kk * a
        kd_ref[z] = kd
        bonus = bonus + _dot_exact_rhs(r * kd * vec[6:7, :], mm) * vr
    bonus_ref[...] = bonus


def _rwkv_prep(rw, mu, vec, up, gup, *, tm, ctx_tiles, ctx_tiles_per_seq, lat_tiles_per_seq):
    t = rw.shape[0]
    hb = tm // 8
    nhb = t // 8
    row = lambda: pl.BlockSpec((tm, D_RWKV), lambda i: (i, 0))
    two = lambda: pl.BlockSpec((2, tm, D_RWKV), lambda i: (0, i, 0))
    full = lambda a: pl.BlockSpec(a.shape, lambda i: (0,) * a.ndim)
    body = functools.partial(_rwkv_prep_body, tm=tm, ctx_tiles=ctx_tiles, ctx_tiles_per_seq=ctx_tiles_per_seq,
                             lat_tiles_per_seq=lat_tiles_per_seq)
    one = jax.ShapeDtypeStruct((t, D_RWKV), F32)
    both = jax.ShapeDtypeStruct((2, t, D_RWKV), F32)
    return pl.pallas_call(
        body,
        grid=(t // tm,),
        in_specs=[pl.BlockSpec((tm, RW_W), lambda i: (i, 0)),
                  pl.BlockSpec((8, RW_W), lambda i: (jnp.maximum(i * hb - 1, 0), 0)),
                  pl.BlockSpec((8, RW_W), lambda i: (jnp.minimum((i + 1) * hb, nhb - 1), 0)),
                  full(mu), full(vec), full(up), full(gup)],
        out_specs=[row(), row(), row(), two(), two(), two(), row(), row()],
        out_shape=[one, one, one, both, both, both, one, one],
        compiler_params=_params("parallel"),
        name="rwkv_prep",
    )(rw, rw, rw, mu, vec, up, gup)


def _unit_lower_inverse(lmat, rowi, coli):
    eye = jnp.where(rowi == coli, 1.0, 0.0).astype(F32)
    n1 = jnp.where((rowi >> 3) == (coli >> 3), lmat, 0.0)
    n2 = _bdot(n1, n1)
    n4 = _bdot(n2, n2)
    t = _bdot(_bdot(eye - n1, eye + n2), eye + n4)
    for sh in (4, 5, 6):
        inner = (rowi >> (sh - 1)) == (coli >> (sh - 1))
        outer = (rowi >> sh) == (coli >> sh)
        e = jnp.where(jnp.logical_and(outer, jnp.logical_not(inner)), lmat, 0.0)
        tb = t.astype(BF16)
        t = t - _bdot(tb, _bdot(e, tb))
    return t


def _rwkv_scan_body(r_ref, kk_ref, v_ref, lw_ref, kka_ref, kd_ref, s0_ref, gn_ref, y_ref, sfin_ref, s_ref):
    z = pl.program_id(1)
    c = pl.program_id(2)

    @pl.when(c == 0)
    def _():
        s_ref[...] = s0_ref[0, 0]

    sgn = 1 - 2 * z
    rowi = lax.broadcasted_iota(jnp.int32, (STACK, STACK), 0)
    coli = lax.broadcasted_iota(jnp.int32, (STACK, STACK), 1)
    dlt = ((coli & (CHUNK - 1)) - (rowi & (CHUNK - 1))) * sgn
    strict = dlt < 0
    row2 = lax.broadcasted_iota(jnp.int32, (STACK, 2 * STACK), 0)
    col2 = lax.broadcasted_iota(jnp.int32, (STACK, 2 * STACK), 1)
    incl2 = ((col2 & (CHUNK - 1)) - (row2 & (CHUNK - 1))) * sgn <= 0
    ri = lax.broadcasted_iota(jnp.int32, (CHUNK, CHUNK), 0)
    ci = lax.broadcasted_iota(jnp.int32, (CHUNK, CHUNK), 1)
    cmat = jnp.where((ci - ri) * sgn <= 0, 1.0, 0.0).astype(F32)

    lw = lw_ref[0]
    cum = _dot_exact_lhs(cmat, lw)
    wt = jnp.exp(cum)
    iw = jnp.exp(-cum)
    a = jnp.exp(cum - lw) * kk_ref[...]
    b = kka_ref[0] * iw
    cm = kd_ref[0] * iw
    q = r_ref[...] * wt
    wtot = jnp.exp(jnp.sum(lw, axis=0, keepdims=True))

    aq = jnp.concatenate([_stack_heads(a), _stack_heads(q)], axis=0).astype(BF16)
    bc = jnp.concatenate([_stack_heads(b), _stack_heads(cm)], axis=0).astype(BF16)
    v_s = _stack_heads(v_ref[...]).astype(BF16)
    s = s_ref[...]

    g = _bdot_nt(aq, bc)
    aqs = _bdot_nt(aq, s)
    lab = jnp.where(strict, g[0:STACK, 0:STACK], 0.0)
    lac = jnp.where(strict, g[0:STACK, STACK:], 0.0)
    mq = jnp.where(incl2, g[STACK:, :], 0.0)
    tinv = _unit_lower_inverse(lab, rowi, coli)
    u_s = -_bdot(tinv, aqs[0:STACK] + _bdot(lac, v_s))
    uv = jnp.concatenate([u_s.astype(BF16), v_s], axis=0)
    y_s = aqs[STACK:] + _bdot(mq, uv)
    s_new = (s + _bdot_tn(uv, bc)) * wtot
    s_ref[...] = s_new

    y_ref[0] = _head_groupnorm(_unstack_heads(y_s), gn_ref[...], RWKV_GN_EPS)

    @pl.when(c == pl.num_programs(2) - 1)
    def _():
        sfin_ref[0, 0] = s_new


def _rwkv_scan(r, kk, v, lw, kka, kd, s0, gn, *, nb, n, tok0):
    nc = n // CHUNK
    cb0 = tok0 // CHUNK

    def cidx(b, z, c):
        return cb0 + b * nc + c + z * (nc - 1 - 2 * c)

    one = lambda: pl.BlockSpec((CHUNK, D_RWKV), lambda b, z, c: (cidx(b, z, c), 0))
    per_dir = lambda: pl.BlockSpec((1, CHUNK, D_RWKV), lambda b, z, c: (z, cidx(b, z, c), 0))
    state = lambda: pl.BlockSpec((1, 1, STACK, STACK), lambda b, z, c: (b, z, 0, 0))
    return pl.pallas_call(
        _rwkv_scan_body,
        grid=(nb, 2, nc),
        in_specs=[one(), one(), one(), per_dir(), per_dir(), per_dir(), state(),
                  pl.BlockSpec((1, D_RWKV), lambda b, z, c: (0, 0))],
        out_specs=[pl.BlockSpec((1, CHUNK, D_RWKV), lambda b, z, c: (z, b * nc + c + z * (nc - 1 - 2 * c), 0)),
                   state()],
        out_shape=[jax.ShapeDtypeStruct((2, nb * n, D_RWKV), F32),
                   jax.ShapeDtypeStruct((nb, 2, STACK, STACK), F32)],
        scratch_shapes=[pltpu.VMEM((STACK, STACK), F32)],
        compiler_params=_params("parallel", "parallel", "arbitrary"),
        name="rwkv_scan",
    )(r, kk, v, lw, kka, kd, s0, gn)


def _ret_body(q_ref, k_ref, v_ref, g_ref, qs_ref, ks_ref, cr_ref, sr_ref, lgm_ref, lgl_ref, gn_ref, s0_ref,
              o_ref, sfin_ref, s_ref):
    z = pl.program_id(1)
    c = pl.program_id(2)

    @pl.when(c == 0)
    def _():
        s_ref[...] = s0_ref[0, 0]

    sgn = 1 - 2 * z
    cr = cr_ref[...]
    sr = sr_ref[...]
    q = q_ref[...] * cr + qs_ref[...] * sr
    k = (k_ref[...] * cr + ks_ref[...] * sr) * (RET_DK ** -0.5)
    v = v_ref[...]
    lgl = lgl_ref[0]
    ti = lax.broadcasted_iota(jnp.int32, (CHUNK, D_RET), 0)
    pos = jnp.where(z == 0, ti, CHUNK - 1 - ti).astype(F32)
    xi = jnp.exp(lgl * (pos + 1.0))
    zeta = jnp.exp(lgl * (CHUNK - 1.0 - pos))

    rowi = lax.broadcasted_iota(jnp.int32, (STACK, STACK), 0)
    coli = lax.broadcasted_iota(jnp.int32, (STACK, STACK), 1)
    dist = ((rowi & (CHUNK - 1)) - (coli & (CHUNK - 1))) * sgn
    dmask = jnp.where(dist >= 0, jnp.exp(lgm_ref[0] * jnp.maximum(dist, 0).astype(F32)), 0.0)

    q_s = _stack_heads(q)
    k_s = _stack_heads(k)
    v_s = _stack_heads(v)
    s = s_ref[...]
    scores = _bdot_nt(q_s, k_s) * dmask
    o = _unstack_heads(_bdot(scores, v_s)) + _bdot(q * xi, s)
    s_new = s * jnp.exp(lgl * float(CHUNK)) + _bdot_tn(_stack_heads(k * zeta), v_s)
    s_ref[...] = s_new

    g = g_ref[...]
    o_ref[0] = _head_groupnorm(o, gn_ref[0], RET_GN_EPS) * (g * _sigmoid(g))

    @pl.when(c == pl.num_programs(2) - 1)
    def _():
        sfin_ref[0, 0] = s_new


def _retention(rt, rts, cr, sr, lgm, lgl, gn, s0, *, nb, n, tok0):
    nc = n // CHUNK
    cb0 = tok0 // CHUNK

    def cidx(b, z, c):
        return cb0 + b * nc + c + z * (nc - 1 - 2 * c)

    col = lambda j: pl.BlockSpec((CHUNK, D_RET), lambda b, z, c: (cidx(b, z, c), j))
    state = lambda: pl.BlockSpec((1, 1, STACK, STACK), lambda b, z, c: (b, z, 0, 0))
    return pl.pallas_call(
        _ret_body,
        grid=(nb, 2, nc),
        in_specs=[col(0), col(1), col(2),
                  pl.BlockSpec((CHUNK, D_RET), lambda b, z, c: (cidx(b, z, c), 3 + z)),
                  col(0), col(1),
                  pl.BlockSpec((CHUNK, D_RET), lambda b, z, c: (cidx(b, z, c), 0)),
                  pl.BlockSpec((CHUNK, D_RET), lambda b, z, c: (cidx(b, z, c), 0)),
                  pl.BlockSpec((1, STACK, STACK), lambda b, z, c: (z, 0, 0)),
                  pl.BlockSpec((1, 1, D_RET), lambda b, z, c: (z, 0, 0)),
                  pl.BlockSpec((1, 1, D_RET), lambda b, z, c: (z, 0, 0)),
                  state()],
        out_specs=[pl.BlockSpec((1, CHUNK, D_RET), lambda b, z, c: (z, b * nc + c + z * (nc - 1 - 2 * c), 0)),
                   state()],
        out_shape=[jax.ShapeDtypeStruct((2, nb * n, D_RET), F32),
                   jax.ShapeDtypeStruct((nb, 2, STACK, STACK), F32)],
        scratch_shapes=[pltpu.VMEM((STACK, STACK), F32)],
        compiler_params=_params("parallel", "parallel", "arbitrary"),
        name="retention",
    )(rt, rt, rt, rt, rts, rts, cr, sr, lgm, lgl, gn, s0)


def _mixout_body(x_ref, mod_ref, ng_ref, mla_ref, yr_ref, bonus_ref, gate_ref, ret_ref, w_ref, o_ref):
    m = mod_ref[0]
    rwkv_o = (yr_ref[0] + yr_ref[1] + bonus_ref[...]) * gate_ref[...]
    ret_o = ret_ref[0] + ret_ref[1]
    mixed = (_bdot(mla_ref[...], w_ref[0:D_MLA, :])
             + _bdot(rwkv_o, w_ref[D_MLA:D_MLA + D_RWKV, :])
             + _bdot(ret_o, w_ref[D_MLA + D_RWKV:, :]))
    o_ref[...] = x_ref[...] + m[5:6, :] * _rms(mixed, ng_ref[3:4, :])


def _mixout(x, mod, ng, mla_o, yr, bonus, gate, ret, w_out, grp, tm):
    t = x.shape[0]
    row = lambda w: pl.BlockSpec((tm, w), lambda i: (i, 0))
    two = lambda w: pl.BlockSpec((2, tm, w), lambda i: (0, i, 0))
    return pl.pallas_call(
        _mixout_body,
        grid=(t // tm,),
        in_specs=[row(D_MODEL),
                  pl.BlockSpec((1, N_MOD, D_MODEL), lambda i: (grp(i), 0, 0)),
                  pl.BlockSpec((6, D_MODEL), lambda i: (0, 0)),
                  row(D_MLA), two(D_RWKV), row(D_RWKV), row(D_RWKV), two(D_RET),
                  pl.BlockSpec(w_out.shape, lambda i: (0, 0))],
        out_specs=row(D_MODEL),
        out_shape=jax.ShapeDtypeStruct((t, D_MODEL), F32),
        compiler_params=_params("parallel"),
        name="mixer_out",
    )(x, mod, ng, mla_o, yr, bonus, gate, ret, w_out)


def _rope_perm(d):
    q = d // 4
    base = np.concatenate([np.arange(q, 2 * q), np.arange(0, q)])
    return np.concatenate([base, base + 2 * q])


def _rope_tables(row, col, d):
    half = d // 4
    inv = ROPE_BASE ** (-jnp.arange(half, dtype=F32) / half)
    parts_c, parts_s = [], []
    for pos in (row, col):
        ang = pos.astype(F32)[:, None] * inv
        cos, sin = jnp.cos(ang), jnp.sin(ang)
        parts_c += [cos, cos]
        parts_s += [-sin, sin]
    return jnp.concatenate(parts_c, axis=-1), jnp.concatenate(parts_s, axis=-1)


def _layout_w_in(w_in):
    depth = w_in.shape[0]
    zeros = lambda w: jnp.zeros((depth, D_MODEL, w), w_in.dtype)
    o = 0
    qa = w_in[..., o:o + Q_LORA]; o += Q_LORA
    kva = w_in[..., o:o + KV_LORA]; o += KV_LORA
    kr = w_in[..., o:o + ROPE_DIM]; o += ROPE_DIM
    rw = w_in[..., o:o + 3 * D_RWKV + 4 * LORA_WA + LORA_G]; o += 3 * D_RWKV + 4 * LORA_WA + LORA_G
    rt = w_in[..., o:]
    kr_blk = lambda x: jnp.concatenate([zeros(ROPE_OFF), x, zeros(HEAD_PAD - ROPE_OFF - ROPE_DIM)], axis=-1)
    rw_parts = [rw[..., :3 * D_RWKV]]
    for j in range(4):
        lo = 3 * D_RWKV + j * LORA_WA
        rw_parts += [rw[..., lo:lo + LORA_WA], zeros(LANE - LORA_WA)]
    rw_parts.append(rw[..., 3 * D_RWKV + 4 * LORA_WA:])
    perm = np.concatenate([h * RET_DK + _rope_perm(RET_DK) for h in range(2 * H_RET)])
    cols = [qa, kva, kr_blk(kr), kr_blk(kr[..., _rope_perm(ROPE_DIM)])] + rw_parts + [rt, rt[..., perm]]
    return jnp.concatenate(cols, axis=-1).astype(BF16)


def _layout_mu(mu):
    depth = mu.shape[0]
    parts = [mu[:, :3 * D_RWKV]]
    for j in range(4):
        lo = 3 * D_RWKV + j * LORA_WA
        parts += [mu[:, lo:lo + LORA_WA], jnp.zeros((depth, LANE - LORA_WA), mu.dtype)]
    parts.append(mu[:, 3 * D_RWKV + 4 * LORA_WA:])
    return jnp.concatenate(parts, axis=-1).reshape(depth, 1, RW_W)


def _layout_wq(wq):
    depth = wq.shape[0]
    w = wq.reshape(depth, Q_LORA, H_MLA, QK_NOPE + ROPE_DIM)
    pad = jnp.zeros((depth, Q_LORA, H_MLA, HEAD_PAD - QK_NOPE - ROPE_DIM), wq.dtype)
    nope, rope = w[..., :QK_NOPE], w[..., QK_NOPE:]
    main = jnp.concatenate([nope, rope, pad], axis=-1)
    swap = jnp.concatenate([jnp.zeros_like(nope), rope[..., _rope_perm(ROPE_DIM)], pad], axis=-1)
    shape = (depth, Q_LORA, H_MLA * HEAD_PAD)
    return main.reshape(shape).astype(BF16), swap.reshape(shape).astype(BF16)


def _layout_wkv(wkv):
    depth = wkv.shape[0]
    w = wkv.reshape(depth, KV_LORA, H_MLA, QK_NOPE + V_HD)
    kn = jnp.concatenate([w[..., :QK_NOPE], jnp.zeros((depth, KV_LORA, H_MLA, HEAD_PAD - QK_NOPE), wkv.dtype)], axis=-1)
    return (kn.reshape(depth, KV_LORA, H_MLA * HEAD_PAD).astype(BF16),
            w[..., QK_NOPE:].reshape(depth, KV_LORA, D_MLA).astype(BF16))


def _layout_lora_up(up):
    depth = up.shape[0]
    return jnp.concatenate([up, jnp.zeros((depth, 4, LANE - LORA_WA, D_RWKV), up.dtype)], axis=2).astype(BF16)


def _block_diag(s):
    eye = jnp.eye(H_RWKV, dtype=s.dtype)
    out = jnp.einsum('...hij,hg->...higj', s, eye)
    return out.reshape(s.shape[:-3] + (STACK, STACK))


def _diag_blocks(s):
    s5 = s.reshape(s.shape[:-2] + (H_RWKV, RWKV_HD, H_RWKV, RWKV_HD))
    return jnp.stack([s5[..., h, :, h, :] for h in range(H_RWKV)], axis=-3)


def kernel(x_prompt, x_sample, cache_mla_ckv, cache_mla_krope, state_rwkv, state_ret, c, c_ctx,
           norm_g, w_mod, b_mod, ffn_wi, ffn_wo, w_in, w_out, mla_norm_q, mla_norm_kv, mla_wq_up,
           mla_wkv_up, rwkv_mu, rwkv_vec, rwkv_lora_up, rwkv_g_up, ret_gn):
    bc, nc, _ = x_prompt.shape
    bl, nl, _ = x_sample.shape
    depth = norm_g.shape[0]
    past = cache_mla_ckv.shape[2]
    tc, tl = bc * nc, bl * nl
    t_all = tc + tl
    assert bl <= 7 and nl % GRID_W == 0
    assert nc % CHUNK == 0 and nl % CHUNK == 0 and tc % nl == 0 and tc % past == 0

    def tile(cap):
        tm = cap
        while tc % tm or nl % tm:
            tm //= 2
        return tm

    def grouper(tm):
        ctx_tiles, per_seq = tc // tm, nl // tm
        return lambda i: jnp.where(i < ctx_tiles, 0, 1 + jnp.maximum(i - ctx_tiles, 0) // per_seq)

    tm_ffn, tm_tok = tile(1024), tile(512)
    tm_shift = 256
    while nc % tm_shift or nl % tm_shift:
        tm_shift //= 2
    tq_ctx, tq_lat = min(nc, 256), min(nl, 256)

    wi_b, wo_b = ffn_wi.astype(BF16), ffn_wo.astype(BF16)
    w_in_b = _layout_w_in(w_in)
    w_out_b = w_out.astype(BF16)
    wq_b, wqs_b = _layout_wq(mla_wq_up)
    wk_b, wv_b = _layout_wkv(mla_wkv_up)
    mu_l = _layout_mu(rwkv_mu)
    up_b = _layout_lora_up(rwkv_lora_up)
    gup_b = rwkv_g_up.astype(BF16)

    pos = jnp.arange(nl)
    row_l, col_l = pos // GRID_W, pos % GRID_W
    cos_m, sin_m = _rope_tables(row_l, col_l, ROPE_DIM)
    cos_r, sin_r = _rope_tables(row_l, col_l, RET_DK)

    def mla_table(tab, fill):
        lat = jnp.concatenate([jnp.full((nl, ROPE_OFF), fill, F32), tab,
                               jnp.zeros((nl, HEAD_PAD - ROPE_OFF - ROPE_DIM), F32)], axis=-1)
        ctx = jnp.concatenate([jnp.full((tc, ROPE_OFF + ROPE_DIM), fill, F32),
                               jnp.zeros((tc, HEAD_PAD - ROPE_OFF - ROPE_DIM), F32)], axis=-1)
        return jnp.concatenate([ctx, jnp.tile(lat, (bl, 1))], axis=0)

    def ret_table(tab, fill):
        return jnp.concatenate([jnp.full((tc, D_RET), fill, F32), jnp.tile(jnp.tile(tab, (1, H_RET)), (bl, 1))], axis=0)

    cq, sq = mla_table(cos_m, 1.0), mla_table(sin_m, 0.0)
    cr, sr = ret_table(cos_r, 1.0), ret_table(sin_r, 0.0)

    e = 5.0 + jnp.arange(H_RET, dtype=F32)[None, :] + 0.5 * jnp.arange(2, dtype=F32)[:, None]
    lg = jnp.log1p(-jnp.exp2(-e))
    lg_lane = jnp.repeat(lg, RET_DK, axis=1)
    lgm = jnp.broadcast_to(lg_lane[:, :, None], (2, STACK, STACK))
    lgl = lg_lane[:, None, :]

    c_all = jnp.concatenate([c_ctx[None, :], c, jnp.zeros((7 - bl, D_MODEL), F32)], axis=0)
    mod = _modulation(c_all, w_mod, b_mod).reshape(depth, 8, N_MOD, D_MODEL)

    kr_cache = jnp.pad(cache_mla_krope, ((0, 0), (0, 0), (0, 0), (ROPE_OFF, HEAD_PAD - ROPE_OFF - ROPE_DIM)))
    zero_state = jnp.zeros((bc, 2, STACK, STACK), F32)
    s0_rwkv_lat = _block_diag(state_rwkv.astype(F32))
    s0_ret_lat = _block_diag(state_ret.astype(F32))

    x = jnp.concatenate([x_prompt.reshape(tc, D_MODEL), x_sample.reshape(tl, D_MODEL)], axis=0)
    ckv_l, krope_l, rwkv_l, ret_l = [], [], [], []
    for l in range(depth):
        mod_l, ng = mod[l], norm_g[l]
        x = _ffn(x, mod_l, ng, wi_b, wo_b, l, 0, grouper(tm_ffn), tm_ffn)
        qa, kva, kr, krs, rw, rt, rts = _mixin(x, mod_l, ng, w_in_b[l], grouper(tm_tok), tm_tok)

        qh, ckv, kh, vv = _mla_prep(qa, kva, kr, krs, cq, sq, mla_norm_q[l][None, :], mla_norm_kv[l][None, :],
                                    wq_b[l], wqs_b[l], wk_b[l], wv_b[l], tm_tok)
        kh_c, v_c = _cache_kv(cache_mla_ckv[:, l].reshape(bl * past, KV_LORA),
                              kr_cache[:, l].reshape(bl * past, HEAD_PAD), wk_b[l], wv_b[l])
        mla_ctx = _attention(qh, kh, vv, None, None, nb=bc, n=nc, m1=nc, tok0=0, k1_tok0=0, tq=tq_ctx)
        mla_lat = _attention(qh, kh_c, v_c, kh, vv, nb=bl, n=nl, m1=past, tok0=tc, k1_tok0=0, tq=tq_lat)
        mla_o = jnp.concatenate([mla_ctx, mla_lat], axis=0)

        r, kk, v, lw, kka, kd, bonus, gate = _rwkv_prep(
            rw, mu_l[l], rwkv_vec[l], up_b[l], gup_b[l],
            tm=tm_shift, ctx_tiles=tc // tm_shift, ctx_tiles_per_seq=nc // tm_shift,
            lat_tiles_per_seq=nl // tm_shift)
        gn_r = rwkv_vec[l][7:8, :]
        y_ctx, s_rwkv = _rwkv_scan(r, kk, v, lw, kka, kd, zero_state, gn_r, nb=bc, n=nc, tok0=0)
        y_lat, _ = _rwkv_scan(r, kk, v, lw, kka, kd, s0_rwkv_lat[:, l], gn_r, nb=bl, n=nl, tok0=tc)
        yr = jnp.concatenate([y_ctx, y_lat], axis=1)

        gn_t = ret_gn[l][:, None, :]
        o_ctx, s_ret = _retention(rt, rts, cr, sr, lgm, lgl, gn_t, zero_state, nb=bc, n=nc, tok0=0)
        o_lat, _ = _retention(rt, rts, cr, sr, lgm, lgl, gn_t, s0_ret_lat[:, l], nb=bl, n=nl, tok0=tc)
        ret = jnp.concatenate([o_ctx, o_lat], axis=1)

        x = _mixout(x, mod_l, ng, mla_o, yr, bonus, gate, ret, w_out_b[l], grouper(tm_tok), tm_tok)
        x = _ffn(x, mod_l, ng, wi_b, wo_b, l, 1, grouper(tm_ffn), tm_ffn)

        ckv_l.append(ckv[:tc].reshape(bc, nc, KV_LORA))
        krope_l.append(kr[:tc, ROPE_OFF:ROPE_OFF + ROPE_DIM].reshape(bc, nc, ROPE_DIM))
        rwkv_l.append(_diag_blocks(s_rwkv))
        ret_l.append(_diag_blocks(s_ret))

    return (x[:tc].reshape(bc, nc, D_MODEL), x[tc:].reshape(bl, nl, D_MODEL),
            jnp.stack(ckv_l, axis=1), jnp.stack(krope_l, axis=1),
            jnp.stack(rwkv_l, axis=1), jnp.stack(ret_l, axis=1))
```

```python
import functools

import numpy as np
import jax
import jax.numpy as jnp
from jax import lax
from jax.experimental import pallas as pl
from jax.experimental.pallas import tpu as pltpu

F32 = jnp.float32
BF16 = jnp.bfloat16

D_MODEL = 1024
N_MOD = 9
D_FF = 2816
H_MLA = 8
QK_NOPE = 64
ROPE_DIM = 32
V_HD = 64
Q_LORA = 256
KV_LORA = 128
D_MLA = H_MLA * V_HD
MLA_SCALE = (QK_NOPE + ROPE_DIM) ** -0.5
H_RWKV = 4
RWKV_HD = 64
D_RWKV = H_RWKV * RWKV_HD
LORA_WA = 64
LORA_G = 128
DECAY_SCALE = 0.6065306597126334
RWKV_GN_EPS = 64e-5
H_RET = 4
RET_DK = 64
D_RET = H_RET * 64
RET_GN_EPS = 1e-5
ROPE_BASE = 10000.0
NORM_EPS = 1e-6
GRID_W = 64

LANE = 128
HEAD_PAD = 128
ROPE_OFF = QK_NOPE
CHUNK = 64
STACK = H_RWKV * CHUNK
PAIR = 2
ATTN_HEADS = 4
VMEM_LIMIT = 56 * 1024 * 1024

C_QA = 0
C_KVA = C_QA + Q_LORA
C_KR = C_KVA + KV_LORA
C_KRS = C_KR + LANE
C_RW = C_KRS + LANE
RW_W = 3 * D_RWKV + 4 * LANE + LORA_G
C_RT = C_RW + RW_W
RT_W = 2 * H_RET * RET_DK + 3 * D_RET
C_RTS = C_RT + RT_W
RTS_W = 2 * H_RET * RET_DK
IN_W = C_RTS + RTS_W


def _params(*sem):
    return pltpu.CompilerParams(dimension_semantics=sem, vmem_limit_bytes=VMEM_LIMIT)


def _sigmoid(x):
    return 1.0 / (1.0 + jnp.exp(-x))


def _rms(x, g):
    return x * lax.rsqrt(jnp.mean(x * x, axis=-1, keepdims=True) + NORM_EPS) * g


def _bdot(a, b):
    return jnp.dot(a.astype(BF16), b.astype(BF16), preferred_element_type=F32)


def _bdot_nt(a, b):
    return lax.dot_general(a.astype(BF16), b.astype(BF16), (((1,), (1,)), ((), ())),
                           preferred_element_type=F32)


def _bdot_tn(a, b):
    return lax.dot_general(a.astype(BF16), b.astype(BF16), (((0,), (0,)), ((), ())),
                           preferred_element_type=F32)


def _split_bf16(x):
    hi = x.astype(BF16)
    return hi, (x - hi.astype(F32)).astype(BF16)


def _dot_exact_rhs(x, m):
    hi, lo = _split_bf16(x)
    mb = m.astype(BF16)
    return jnp.dot(hi, mb, preferred_element_type=F32) + jnp.dot(lo, mb, preferred_element_type=F32)


def _dot_exact_lhs(m, x):
    hi, lo = _split_bf16(x)
    mb = m.astype(BF16)
    return jnp.dot(mb, hi, preferred_element_type=F32) + jnp.dot(mb, lo, preferred_element_type=F32)


def _head_mean_matrix():
    r = lax.broadcasted_iota(jnp.int32, (STACK, STACK), 0) // RWKV_HD
    c = lax.broadcasted_iota(jnp.int32, (STACK, STACK), 1) // RWKV_HD
    return jnp.where(r == c, 1.0 / RWKV_HD, 0.0).astype(F32)


def _head_groupnorm(y, gain, eps):
    mm = _head_mean_matrix()
    yc = y - _dot_exact_rhs(y, mm)
    var = _dot_exact_rhs(yc * yc, mm)
    return yc * lax.rsqrt(var + eps) * gain


def _stack_heads(x):
    head = lax.broadcasted_iota(jnp.int32, x.shape, 1) // RWKV_HD
    return jnp.concatenate([jnp.where(head == h, x, 0.0) for h in range(H_RWKV)], axis=0)


def _unstack_heads(x):
    out = x[0:CHUNK]
    for h in range(1, H_RWKV):
        out = out + x[h * CHUNK:(h + 1) * CHUNK]
    return out


def _mod_body(c_ref, w_ref, b_ref, o_ref):
    c = c_ref[...]
    s = c * _sigmoid(c)
    o_ref[0] = _bdot(s, w_ref[0]) + b_ref[0]


def _modulation(c_all, w_mod, b_mod):
    depth, d, nd = w_mod.shape
    tn = D_MODEL
    return pl.pallas_call(
        _mod_body,
        grid=(depth, nd // tn),
        in_specs=[pl.BlockSpec((8, d), lambda l, j: (0, 0)),
                  pl.BlockSpec((1, d, tn), lambda l, j: (l, 0, j)),
                  pl.BlockSpec((1, 1, tn), lambda l, j: (l, 0, j))],
        out_specs=pl.BlockSpec((1, 8, tn), lambda l, j: (l, 0, j)),
        out_shape=jax.ShapeDtypeStruct((depth, 8, nd), F32),
        compiler_params=_params("parallel", "parallel"),
        name="modulation",
    )(c_all, w_mod, b_mod.reshape(depth, 1, nd))


def _ffn_body(x_ref, mod_ref, ng_ref, wg_ref, wu_ref, wo_ref, o_ref, h_ref, acc_ref, *, mrow, grow):
    f = pl.program_id(1)

    @pl.when(f == 0)
    def _():
        m = mod_ref[0]
        h = _rms(x_ref[...], ng_ref[grow:grow + 1, :]) * (1.0 + m[mrow + 1:mrow + 2, :]) + m[mrow:mrow + 1, :]
        h_ref[...] = h.astype(BF16)
        acc_ref[...] = jnp.zeros_like(acc_ref)

    h = h_ref[...]
    g = jnp.dot(h, wg_ref[...], preferred_element_type=F32)
    u = jnp.dot(h, wu_ref[...], preferred_element_type=F32)
    a = (g * _sigmoid(g)) * u
    acc_ref[...] += jnp.dot(a.astype(BF16), wo_ref[...], preferred_element_type=F32)

    @pl.when(f == pl.num_programs(1) - 1)
    def _():
        m = mod_ref[0]
        o_ref[...] = x_ref[...] + 0.5 * m[mrow + 2:mrow + 3, :] * _rms(acc_ref[...], ng_ref[grow + 1:grow + 2, :])


def _ffn(x, mod, ng, wi, wo, layer, which, grp, tm):
    t = x.shape[0]
    tf = 256
    nf = D_FF // tf
    body = functools.partial(_ffn_body, mrow=6 * which, grow=4 * which)
    return pl.pallas_call(
        body,
        grid=(t // tm, nf),
        in_specs=[pl.BlockSpec((tm, D_MODEL), lambda i, f: (i, 0)),
                  pl.BlockSpec((1, N_MOD, D_MODEL), lambda i, f: (grp(i), 0, 0)),
                  pl.BlockSpec((6, D_MODEL), lambda i, f: (0, 0)),
                  pl.BlockSpec((None, None, D_MODEL, tf), lambda i, f: (layer, which, 0, f)),
                  pl.BlockSpec((None, None, D_MODEL, tf), lambda i, f: (layer, which, 0, f + nf)),
                  pl.BlockSpec((None, None, tf, D_MODEL), lambda i, f: (layer, which, f, 0))],
        out_specs=pl.BlockSpec((tm, D_MODEL), lambda i, f: (i, 0)),
        out_shape=jax.ShapeDtypeStruct((t, D_MODEL), F32),
        scratch_shapes=[pltpu.VMEM((tm, D_MODEL), BF16), pltpu.VMEM((tm, D_MODEL), F32)],
        compiler_params=_params("parallel", "arbitrary"),
        name="ffn",
    )(x, mod, ng, wi, wi, wo)


def _mixin_body(x_ref, mod_ref, ng_ref, w_ref, qa_ref, kva_ref, kr_ref, krs_ref, rw_ref, rt_ref, rts_ref):
    m = mod_ref[0]
    h = (_rms(x_ref[...], ng_ref[2:3, :]) * (1.0 + m[4:5, :]) + m[3:4, :]).astype(BF16)

    def proj(lo, width):
        return jnp.dot(h, w_ref[:, lo:lo + width], preferred_element_type=F32)

    qa_ref[...] = proj(C_QA, Q_LORA)
    kva_ref[...] = proj(C_KVA, KV_LORA)
    kr_ref[...] = proj(C_KR, LANE)
    krs_ref[...] = proj(C_KRS, LANE)
    rw_ref[...] = proj(C_RW, RW_W)
    rt_ref[...] = proj(C_RT, RT_W)
    rts_ref[...] = proj(C_RTS, RTS_W)


def _mixin(x, mod, ng, w_in, grp, tm):
    t = x.shape[0]
    widths = (Q_LORA, KV_LORA, LANE, LANE, RW_W, RT_W, RTS_W)
    return pl.pallas_call(
        _mixin_body,
        grid=(t // tm,),
        in_specs=[pl.BlockSpec((tm, D_MODEL), lambda i: (i, 0)),
                  pl.BlockSpec((1, N_MOD, D_MODEL), lambda i: (grp(i), 0, 0)),
                  pl.BlockSpec((6, D_MODEL), lambda i: (0, 0)),
                  pl.BlockSpec((D_MODEL, IN_W), lambda i: (0, 0))],
        out_specs=[pl.BlockSpec((tm, w), lambda i: (i, 0)) for w in widths],
        out_shape=[jax.ShapeDtypeStruct((t, w), F32) for w in widths],
        compiler_params=_params("parallel"),
        name="mixer_in",
    )(x, mod, ng, w_in)


def _kv_expand(ckv, krr, wk_ref, wv_ref, kh_ref, v_ref):
    c = ckv.astype(BF16)
    kn = jnp.dot(c, wk_ref[...], preferred_element_type=F32)
    v = jnp.dot(c, wv_ref[...], preferred_element_type=F32)
    for h in range(H_MLA):
        kh_ref[h] = (kn[:, h * HEAD_PAD:(h + 1) * HEAD_PAD] + krr).astype(BF16)
    for p in range(H_MLA // 2):
        v_ref[p] = v[:, p * LANE:(p + 1) * LANE].astype(BF16)


def _mla_prep_body(qa_ref, kva_ref, kr_ref, krs_ref, cq_ref, sq_ref, gq_ref, gkv_ref, wq_ref, wqs_ref,
                   wk_ref, wv_ref, qh_ref, ckv_ref, kh_ref, v_ref):
    cq = cq_ref[...]
    sq = sq_ref[...]
    qn = _rms(qa_ref[...], gq_ref[...]).astype(BF16)
    q = jnp.dot(qn, wq_ref[...], preferred_element_type=F32)
    qs = jnp.dot(qn, wqs_ref[...], preferred_element_type=F32)
    for h in range(H_MLA):
        sl = slice(h * HEAD_PAD, (h + 1) * HEAD_PAD)
        qh_ref[h] = (q[:, sl] * cq + qs[:, sl] * sq).astype(BF16)
    ckv = _rms(kva_ref[...], gkv_ref[...])
    ckv_ref[...] = ckv
    krr = kr_ref[...] * cq + krs_ref[...] * sq
    _kv_expand(ckv, krr, wk_ref, wv_ref, kh_ref, v_ref)


def _mla_prep(qa, kva, kr, krs, cq, sq, gq, gkv, wq, wqs, wk, wv, tm):
    t = qa.shape[0]
    row = lambda w: pl.BlockSpec((tm, w), lambda i: (i, 0))
    full = lambda a: pl.BlockSpec(a.shape, lambda i: (0,) * a.ndim)
    return pl.pallas_call(
        _mla_prep_body,
        grid=(t // tm,),
        in_specs=[row(Q_LORA), row(KV_LORA), row(LANE), row(LANE), row(LANE), row(LANE),
                  full(gq), full(gkv), full(wq), full(wqs), full(wk), full(wv)],
        out_specs=[pl.BlockSpec((H_MLA, tm, HEAD_PAD), lambda i: (0, i, 0)),
                   row(KV_LORA),
                   pl.BlockSpec((H_MLA, tm, HEAD_PAD), lambda i: (0, i, 0)),
                   pl.BlockSpec((H_MLA // 2, tm, LANE), lambda i: (0, i, 0))],
        out_shape=[jax.ShapeDtypeStruct((H_MLA, t, HEAD_PAD), BF16),
                   jax.ShapeDtypeStruct((t, KV_LORA), F32),
                   jax.ShapeDtypeStruct((H_MLA, t, HEAD_PAD), BF16),
                   jax.ShapeDtypeStruct((H_MLA // 2, t, LANE), BF16)],
        compiler_params=_params("parallel"),
        name="mla_prep",
    )(qa, kva, kr, krs, cq, sq, gq, gkv, wq, wqs, wk, wv)


def _cache_kv_body(ckv_ref, kr_ref, wk_ref, wv_ref, kh_ref, v_ref):
    _kv_expand(ckv_ref[...], kr_ref[...], wk_ref, wv_ref, kh_ref, v_ref)


def _cache_kv(ckv, kr, wk, wv):
    t = ckv.shape[0]
    tm = min(t, 512)
    row = lambda w: pl.BlockSpec((tm, w), lambda i: (i, 0))
    full = lambda a: pl.BlockSpec(a.shape, lambda i: (0,) * a.ndim)
    return pl.pallas_call(
        _cache_kv_body,
        grid=(t // tm,),
        in_specs=[row(KV_LORA), row(LANE), full(wk), full(wv)],
        out_specs=[pl.BlockSpec((H_MLA, tm, HEAD_PAD), lambda i: (0, i, 0)),
                   pl.BlockSpec((H_MLA // 2, tm, LANE), lambda i: (0, i, 0))],
        out_shape=[jax.ShapeDtypeStruct((H_MLA, t, HEAD_PAD), BF16),
                   jax.ShapeDtypeStruct((H_MLA // 2, t, LANE), BF16)],
        compiler_params=_params("parallel"),
        name="cache_kv",
    )(ckv, kr, wk, wv)


def _attn_body(*refs, two):
    if two:
        q_ref, k1_ref, v1_ref, k2_ref, v2_ref, o_ref = refs
    else:
        q_ref, k1_ref, v1_ref, o_ref = refs
    nt = (((1,), (1,)), ((), ()))
    krefs = (k1_ref, k2_ref) if two else (k1_ref,)
    vrefs = (v1_ref, v2_ref) if two else (v1_ref,)

    def scores(j):
        return [lax.dot_general(q_ref[j], k[j], nt, preferred_element_type=F32) * MLA_SCALE for k in krefs]

    def softmax(ss):
        m = functools.reduce(jnp.maximum, [jnp.max(s, axis=-1, keepdims=True) for s in ss])
        ps = [jnp.exp(s - m) for s in ss]
        return [p.astype(BF16) for p in ps], sum(jnp.sum(p, axis=-1, keepdims=True) for p in ps)

    def weighted(j, ps, l):
        return sum(jnp.dot(p, v[j // 2], preferred_element_type=F32) for p, v in zip(ps, vrefs)) / l

    outs, ss, pl_ = [], {}, {}
    for step in range(ATTN_HEADS + 2):
        if step < ATTN_HEADS:
            ss[step] = scores(step)
        if 1 <= step <= ATTN_HEADS:
            pl_[step - 1] = softmax(ss.pop(step - 1))
        if step >= 2:
            outs.append(weighted(step - 2, *pl_.pop(step - 2)))
    lane = lax.broadcasted_iota(jnp.int32, outs[0].shape, 1)
    for p in range(ATTN_HEADS // 2):
        o_ref[:, p * LANE:(p + 1) * LANE] = jnp.where(lane < V_HD, outs[2 * p], outs[2 * p + 1])


def _attention(qh, kh1, v1, kh2, v2, *, nb, n, m1, tok0, k1_tok0, tq):
    two = kh2 is not None
    nq = n // tq
    qb0 = tok0 // tq
    hp, vp = ATTN_HEADS, ATTN_HEADS // 2
    in_specs = [pl.BlockSpec((hp, tq, HEAD_PAD), lambda b, p, i: (p, qb0 + b * nq + i, 0)),
                pl.BlockSpec((hp, m1, HEAD_PAD), lambda b, p, i: (p, k1_tok0 // m1 + b, 0)),
                pl.BlockSpec((vp, m1, LANE), lambda b, p, i: (p, k1_tok0 // m1 + b, 0))]
    args = [qh, kh1, v1]
    if two:
        in_specs += [pl.BlockSpec((hp, n, HEAD_PAD), lambda b, p, i: (p, tok0 // n + b, 0)),
                     pl.BlockSpec((vp, n, LANE), lambda b, p, i: (p, tok0 // n + b, 0))]
        args += [kh2, v2]
    return pl.pallas_call(
        functools.partial(_attn_body, two=two),
        grid=(nb, H_MLA // hp, nq),
        in_specs=in_specs,
        out_specs=pl.BlockSpec((tq, vp * LANE), lambda b, p, i: (b * nq + i, p)),
        out_shape=jax.ShapeDtypeStruct((nb * n, D_MLA), F32),
        compiler_params=_params("parallel", "parallel", "parallel"),
        name="mla_attention",
    )(*args)


def _rwkv_prep_body(x_ref, xp_ref, xn_ref, mu_ref, vec_ref, up_ref, gup_ref,
                    r_ref, kk_ref, v_ref, lw_ref, kka_ref, kd_ref, bonus_ref, gate_ref,
                    *, tm, ctx_tiles, ctx_tiles_per_seq, lat_tiles_per_seq):
    i = pl.program_id(0)
    in_ctx = i < ctx_tiles
    per_seq = jnp.where(in_ctx, ctx_tiles_per_seq, lat_tiles_per_seq)
    j = jnp.where(in_ctx, i % ctx_tiles_per_seq, jnp.maximum(i - ctx_tiles, 0) % lat_tiles_per_seq)
    first = j == 0
    last = j == per_seq - 1
    x = x_ref[...]
    rowi = lax.broadcasted_iota(jnp.int32, x.shape, 0)
    prev_row = jnp.where(first, 0.0, xp_ref[7:8, :])
    next_row = jnp.where(last, 0.0, xn_ref[0:1, :])
    xprev = jnp.where(rowi == 0, prev_row, pltpu.roll(x, 1, 0))
    xnext = jnp.where(rowi == tm - 1, next_row, pltpu.roll(x, tm - 1, 0))
    xr = x + mu_ref[...] * (0.5 * (xprev + xnext) - x)

    vec = vec_ref[...]
    r = xr[:, 0:D_RWKV]
    kr = xr[:, D_RWKV:2 * D_RWKV]
    vr = xr[:, 2 * D_RWKV:3 * D_RWKV]
    o = 3 * D_RWKV
    zw = (xr[:, o:o + LANE], xr[:, o + LANE:o + 2 * LANE])
    za = (xr[:, o + 2 * LANE:o + 3 * LANE], xr[:, o + 3 * LANE:o + 4 * LANE])
    zg = xr[:, o + 4 * LANE:o + 5 * LANE]

    mm = _head_mean_matrix() * float(RWKV_HD)
    kk = kr * vec[4:5, :]
    kk = kk / (jnp.sqrt(_dot_exact_rhs(kk * kk, mm)) + 1e-12)
    r_ref[...] = r
    kk_ref[...] = kk
    v_ref[...] = vr
    gate_ref[...] = _bdot(_sigmoid(zg), gup_ref[...])
    bonus = jnp.zeros_like(r)
    for z in range(2):
        zwz = _bdot(jnp.tanh(zw[z]), up_ref[z]) + vec[z:z + 1, :]
        lw_ref[z] = -DECAY_SCALE * _sigmoid(zwz)
        a = _sigmoid(_bdot(za[z], up_ref[2 + z]) + vec[2 + z:3 + z, :])
        kd = kr * (1.0 + (a - 1.0) * vec[5:6, :])
        kka_ref[z] = kk * a
        kd_ref[z] = kd
        bonus = bonus + _dot_exact_rhs(r * kd * vec[6:7, :], mm) * vr
    bonus_ref[...] = bonus


def _rwkv_prep(rw, mu, vec, up, gup, *, tm, ctx_tiles, ctx_tiles_per_seq, lat_tiles_per_seq):
    t = rw.shape[0]
    hb = tm // 8
    nhb = t // 8
    row = lambda: pl.BlockSpec((tm, D_RWKV), lambda i: (i, 0))
    two = lambda: pl.BlockSpec((2, tm, D_RWKV), lambda i: (0, i, 0))
    full = lambda a: pl.BlockSpec(a.shape, lambda i: (0,) * a.ndim)
    body = functools.partial(_rwkv_prep_body, tm=tm, ctx_tiles=ctx_tiles, ctx_tiles_per_seq=ctx_tiles_per_seq,
                             lat_tiles_per_seq=lat_tiles_per_seq)
    one = jax.ShapeDtypeStruct((t, D_RWKV), F32)
    both = jax.ShapeDtypeStruct((2, t, D_RWKV), F32)
    return pl.pallas_call(
        body,
        grid=(t // tm,),
        in_specs=[pl.BlockSpec((tm, RW_W), lambda i: (i, 0)),
                  pl.BlockSpec((8, RW_W), lambda i: (jnp.maximum(i * hb - 1, 0), 0)),
                  pl.BlockSpec((8, RW_W), lambda i: (jnp.minimum((i + 1) * hb, nhb - 1), 0)),
                  full(mu), full(vec), full(up), full(gup)],
        out_specs=[row(), row(), row(), two(), two(), two(), row(), row()],
        out_shape=[one, one, one, both, both, both, one, one],
        compiler_params=_params("parallel"),
        name="rwkv_prep",
    )(rw, rw, rw, mu, vec, up, gup)


def _unit_lower_inverse(lmats, rowi, coli):
    eye = jnp.where(rowi == coli, 1.0, 0.0).astype(F32)
    n1 = [jnp.where((rowi >> 3) == (coli >> 3), lm, 0.0) for lm in lmats]
    n2 = [_bdot(x, x) for x in n1]
    n4 = [_bdot(x, x) for x in n2]
    t = [_bdot(eye - x, eye + y) for x, y in zip(n1, n2)]
    t = [_bdot(x, eye + y) for x, y in zip(t, n4)]
    for sh in (4, 5, 6):
        inner = (rowi >> (sh - 1)) == (coli >> (sh - 1))
        outer = (rowi >> sh) == (coli >> sh)
        off = jnp.logical_and(outer, jnp.logical_not(inner))
        tb = [x.astype(BF16) for x in t]
        et = [_bdot(jnp.where(off, lm, 0.0), x) for lm, x in zip(lmats, tb)]
        t = [x - _bdot(xb, y) for x, xb, y in zip(t, tb, et)]
    return t


def _scan_masks(z):
    sgn = 1 - 2 * z
    rowi = lax.broadcasted_iota(jnp.int32, (STACK, STACK), 0)
    coli = lax.broadcasted_iota(jnp.int32, (STACK, STACK), 1)
    strict = ((coli & (CHUNK - 1)) - (rowi & (CHUNK - 1))) * sgn < 0
    row2 = lax.broadcasted_iota(jnp.int32, (STACK, 2 * STACK), 0)
    col2 = lax.broadcasted_iota(jnp.int32, (STACK, 2 * STACK), 1)
    incl2 = ((col2 & (CHUNK - 1)) - (row2 & (CHUNK - 1))) * sgn <= 0
    ri = lax.broadcasted_iota(jnp.int32, (CHUNK, CHUNK), 0)
    ci = lax.broadcasted_iota(jnp.int32, (CHUNK, CHUNK), 1)
    cmat = jnp.where((ci - ri) * sgn <= 0, 1.0, 0.0).astype(F32)
    return strict, incl2, cmat


def _rwkv_steps(chains):
    rowi = lax.broadcasted_iota(jnp.int32, (STACK, STACK), 0)
    coli = lax.broadcasted_iota(jnp.int32, (STACK, STACK), 1)
    aq, bc, vs, wtot = [], [], [], []
    for r, kk, v, lw, kka, kd, s, (strict, incl2, cmat) in chains:
        cum = _dot_exact_lhs(cmat, lw)
        iw = jnp.exp(-cum)
        a = jnp.exp(cum - lw) * kk
        q = r * jnp.exp(cum)
        aq.append(jnp.concatenate([_stack_heads(a), _stack_heads(q)], axis=0).astype(BF16))
        bc.append(jnp.concatenate([_stack_heads(kka * iw), _stack_heads(kd * iw)], axis=0).astype(BF16))
        vs.append(_stack_heads(v).astype(BF16))
        wtot.append(jnp.exp(jnp.sum(lw, axis=0, keepdims=True)))
    g = [_bdot_nt(x, y) for x, y in zip(aq, bc)]
    aqs = [_bdot_nt(x, ch[6]) for x, ch in zip(aq, chains)]
    lab = [jnp.where(ch[7][0], x[0:STACK, 0:STACK], 0.0) for x, ch in zip(g, chains)]
    lac = [jnp.where(ch[7][0], x[0:STACK, STACK:], 0.0) for x, ch in zip(g, chains)]
    rhs = [x[0:STACK] + _bdot(y, v) for x, y, v in zip(aqs, lac, vs)]
    tinv = _unit_lower_inverse(lab, rowi, coli)
    uv = [jnp.concatenate([(-_bdot(t, x)).astype(BF16), v], axis=0) for t, x, v in zip(tinv, rhs, vs)]
    out = []
    for ch, gx, ax, uvx, bcx, wx in zip(chains, g, aqs, uv, bc, wtot):
        y_s = ax[STACK:] + _bdot(jnp.where(ch[7][1], gx[STACK:, :], 0.0), uvx)
        s_new = (ch[6] + _bdot_tn(uvx, bcx)) * wx
        out.append((_unstack_heads(y_s), s_new))
    return out


def _rwkv_scan_body(*refs):
    n_in = 6 * 2 * PAIR
    s0_ref, gn_ref, yf_ref, yb_ref, sfin_ref, s_ref = refs[n_in:]
    c = pl.program_id(1)

    @pl.when(c == 0)
    def _():
        s_ref[...] = s0_ref[...]

    masks = [_scan_masks(z) for z in range(2)]
    ids = [(bb, z) for bb in range(PAIR) for z in range(2)]
    chains = []
    for bb, z in ids:
        r_ref, kk_ref, v_ref, lw_ref, kka_ref, kd_ref = refs[6 * (2 * bb + z):6 * (2 * bb + z) + 6]
        chains.append((r_ref[...], kk_ref[...], v_ref[...], lw_ref[0], kka_ref[0], kd_ref[0], s_ref[bb, z], masks[z]))
    results = _rwkv_steps(chains)
    gn = gn_ref[...]
    for (bb, z), (y, s_new) in zip(ids, results):
        s_ref[bb, z] = s_new
        (yf_ref, yb_ref)[z][0, bb] = _head_groupnorm(y, gn, RWKV_GN_EPS)

    @pl.when(c == pl.num_programs(1) - 1)
    def _():
        for (bb, z), (_, s_new) in zip(ids, results):
            sfin_ref[bb, z] = _unstack_heads(s_new)


def _scan_specs(nb, n, tok0):
    nc = n // CHUNK
    cb0 = tok0 // CHUNK

    def cidx(bb, z):
        return lambda i, c: cb0 + (PAIR * i + bb) * nc + c + z * (nc - 1 - 2 * c)

    state = pl.BlockSpec((PAIR, 2, STACK, STACK), lambda i, c: (i, 0, 0, 0))
    out_specs = [pl.BlockSpec((1, PAIR, CHUNK, D_RWKV), lambda i, c: (i, 0, c, 0)),
                 pl.BlockSpec((1, PAIR, CHUNK, D_RWKV), lambda i, c: (i, 0, nc - 1 - c, 0)),
                 pl.BlockSpec((PAIR, 2, RWKV_HD, STACK), lambda i, c: (i, 0, 0, 0))]
    out_shape = [jax.ShapeDtypeStruct((nb // PAIR, PAIR, n, D_RWKV), F32),
                 jax.ShapeDtypeStruct((nb // PAIR, PAIR, n, D_RWKV), F32),
                 jax.ShapeDtypeStruct((nb, 2, RWKV_HD, STACK), F32)]
    return nc, cidx, state, out_specs, out_shape


def _rwkv_scan(r, kk, v, lw, kka, kd, s0, gn, *, nb, n, tok0):
    nc, cidx, state, out_specs, out_shape = _scan_specs(nb, n, tok0)
    in_specs, args = [], []
    for bb in range(PAIR):
        for z in range(2):
            ci = cidx(bb, z)
            for arr in (r, kk, v):
                in_specs.append(pl.BlockSpec((CHUNK, D_RWKV), lambda i, c, ci=ci: (ci(i, c), 0)))
                args.append(arr)
            for arr in (lw, kka, kd):
                in_specs.append(pl.BlockSpec((1, CHUNK, D_RWKV), lambda i, c, ci=ci, z=z: (z, ci(i, c), 0)))
                args.append(arr)
    in_specs += [state, pl.BlockSpec((1, D_RWKV), lambda i, c: (0, 0))]
    y_f, y_b, sfin = pl.pallas_call(
        _rwkv_scan_body,
        grid=(nb // PAIR, nc),
        in_specs=in_specs,
        out_specs=out_specs,
        out_shape=out_shape,
        scratch_shapes=[pltpu.VMEM((PAIR, 2, STACK, STACK), F32)],
        compiler_params=_params("parallel", "arbitrary"),
        name="rwkv_scan",
    )(*args, s0, gn)
    return y_f.reshape(nb * n, D_RWKV), y_b.reshape(nb * n, D_RWKV), sfin


def _ret_consts(z, lgm, lgl):
    sgn = 1 - 2 * z
    ti = lax.broadcasted_iota(jnp.int32, (CHUNK, D_RET), 0)
    pos = (ti if z == 0 else CHUNK - 1 - ti).astype(F32)
    xi = jnp.exp(lgl * (pos + 1.0))
    zeta = jnp.exp(lgl * (CHUNK - 1.0 - pos))
    rowi = lax.broadcasted_iota(jnp.int32, (STACK, STACK), 0)
    coli = lax.broadcasted_iota(jnp.int32, (STACK, STACK), 1)
    dist = ((rowi & (CHUNK - 1)) - (coli & (CHUNK - 1))) * sgn
    dmask = jnp.where(dist >= 0, jnp.exp(lgm * jnp.maximum(dist, 0).astype(F32)), 0.0)
    return dmask, xi, zeta, jnp.exp(lgl * float(CHUNK))


def _ret_steps(chains):
    qr = [q * cr + qs * sr for q, k, v, g, qs, ks, cr, sr, s, cst, gn in chains]
    kr = [(k * cr + ks * sr) * (RET_DK ** -0.5) for q, k, v, g, qs, ks, cr, sr, s, cst, gn in chains]
    vs = [_stack_heads(ch[2]).astype(BF16) for ch in chains]
    scores = [_bdot_nt(_stack_heads(q), _stack_heads(k)) * ch[9][0] for q, k, ch in zip(qr, kr, chains)]
    cross = [_bdot(q * ch[9][1], ch[8]) for q, ch in zip(qr, chains)]
    inner = [_bdot(sc, v) for sc, v in zip(scores, vs)]
    kv = [_bdot_tn(_stack_heads(k * ch[9][2]), v) for k, v, ch in zip(kr, vs, chains)]
    out = []
    for ch, inn, crs, kvx in zip(chains, inner, cross, kv):
        g = ch[3]
        o = _head_groupnorm(_unstack_heads(inn) + crs, ch[10], RET_GN_EPS) * (g * _sigmoid(g))
        out.append((o, ch[8] * ch[9][3] + kvx))
    return out


def _ret_body(*refs):
    n_in = 8 * 2 * PAIR
    lgm_ref, lgl_ref, gn_ref, s0_ref, of_ref, ob_ref, sfin_ref, s_ref = refs[n_in:]
    c = pl.program_id(1)

    @pl.when(c == 0)
    def _():
        s_ref[...] = s0_ref[...]

    consts = [_ret_consts(z, lgm_ref[z], lgl_ref[z]) for z in range(2)]
    ids = [(bb, z) for bb in range(PAIR) for z in range(2)]
    chains = []
    for bb, z in ids:
        ins = [ref[...] for ref in refs[8 * (2 * bb + z):8 * (2 * bb + z) + 8]]
        chains.append((*ins, s_ref[bb, z], consts[z], gn_ref[z]))
    results = _ret_steps(chains)
    for (bb, z), (o, s_new) in zip(ids, results):
        s_ref[bb, z] = s_new
        (of_ref, ob_ref)[z][0, bb] = o

    @pl.when(c == pl.num_programs(1) - 1)
    def _():
        for (bb, z), (_, s_new) in zip(ids, results):
            sfin_ref[bb, z] = _unstack_heads(s_new)


def _retention(rt, rts, cr, sr, lgm, lgl, gn, s0, *, nb, n, tok0):
    nc, cidx, state, out_specs, out_shape = _scan_specs(nb, n, tok0)
    in_specs, args = [], []
    for bb in range(PAIR):
        for z in range(2):
            ci = cidx(bb, z)
            for arr, col in ((rt, 0), (rt, 1), (rt, 2), (rt, 3 + z), (rts, 0), (rts, 1), (cr, 0), (sr, 0)):
                in_specs.append(pl.BlockSpec((CHUNK, D_RET), lambda i, c, ci=ci, col=col: (ci(i, c), col)))
                args.append(arr)
    full = lambda a: pl.BlockSpec(a.shape, lambda i, c: (0,) * a.ndim)
    in_specs += [full(lgm), full(lgl), full(gn), state]
    o_f, o_b, sfin = pl.pallas_call(
        _ret_body,
        grid=(nb // PAIR, nc),
        in_specs=in_specs,
        out_specs=out_specs,
        out_shape=out_shape,
        scratch_shapes=[pltpu.VMEM((PAIR, 2, STACK, STACK), F32)],
        compiler_params=_params("parallel", "arbitrary"),
        name="retention",
    )(*args, lgm, lgl, gn, s0)
    return o_f.reshape(nb * n, D_RET), o_b.reshape(nb * n, D_RET), sfin


def _mixout_body(x_ref, mod_ref, ng_ref, bonus_ref, gate_ref, w_ref, *refs, ctx_tiles):
    ctx_refs, lat_refs, o_ref = refs[0:5], refs[5:10], refs[10]
    in_ctx = pl.program_id(0) < ctx_tiles
    mla, yf, yb, of, ob = [jnp.where(in_ctx, a[...], b[...]) for a, b in zip(ctx_refs, lat_refs)]
    m = mod_ref[0]
    rwkv_o = (yf + yb + bonus_ref[...]) * gate_ref[...]
    ret_o = of + ob
    mixed = (_bdot(mla, w_ref[0:D_MLA, :])
             + _bdot(rwkv_o, w_ref[D_MLA:D_MLA + D_RWKV, :])
             + _bdot(ret_o, w_ref[D_MLA + D_RWKV:, :]))
    o_ref[...] = x_ref[...] + m[5:6, :] * _rms(mixed, ng_ref[3:4, :])


def _mixout(x, mod, ng, bonus, gate, w_out, ctx_parts, lat_parts, grp, tm, ctx_tiles):
    t = x.shape[0]
    row = lambda w: pl.BlockSpec((tm, w), lambda i: (i, 0))
    ctx = lambda a: pl.BlockSpec((tm, a.shape[1]), lambda i: (jnp.minimum(i, ctx_tiles - 1), 0))
    lat = lambda a: pl.BlockSpec((tm, a.shape[1]), lambda i: (jnp.maximum(i - ctx_tiles, 0), 0))
    return pl.pallas_call(
        functools.partial(_mixout_body, ctx_tiles=ctx_tiles),
        grid=(t // tm,),
        in_specs=[row(D_MODEL),
                  pl.BlockSpec((1, N_MOD, D_MODEL), lambda i: (grp(i), 0, 0)),
                  pl.BlockSpec((6, D_MODEL), lambda i: (0, 0)),
                  row(D_RWKV), row(D_RWKV),
                  pl.BlockSpec(w_out.shape, lambda i: (0, 0))]
                 + [ctx(a) for a in ctx_parts] + [lat(a) for a in lat_parts],
        out_specs=row(D_MODEL),
        out_shape=jax.ShapeDtypeStruct((t, D_MODEL), F32),
        compiler_params=_params("parallel"),
        name="mixer_out",
    )(x, mod, ng, bonus, gate, w_out, *ctx_parts, *lat_parts)


def _rope_perm(d):
    q = d // 4
    base = np.concatenate([np.arange(q, 2 * q), np.arange(0, q)])
    return np.concatenate([base, base + 2 * q])


def _rope_tables(row, col, d):
    half = d // 4
    inv = ROPE_BASE ** (-jnp.arange(half, dtype=F32) / half)
    parts_c, parts_s = [], []
    for pos in (row, col):
        ang = pos.astype(F32)[:, None] * inv
        cos, sin = jnp.cos(ang), jnp.sin(ang)
        parts_c += [cos, cos]
        parts_s += [-sin, sin]
    return jnp.concatenate(parts_c, axis=-1), jnp.concatenate(parts_s, axis=-1)


def _layout_w_in(w_in):
    depth = w_in.shape[0]
    zeros = lambda w: jnp.zeros((depth, D_MODEL, w), w_in.dtype)
    o = 0
    qa = w_in[..., o:o + Q_LORA]; o += Q_LORA
    kva = w_in[..., o:o + KV_LORA]; o += KV_LORA
    kr = w_in[..., o:o + ROPE_DIM]; o += ROPE_DIM
    rw = w_in[..., o:o + 3 * D_RWKV + 4 * LORA_WA + LORA_G]; o += 3 * D_RWKV + 4 * LORA_WA + LORA_G
    rt = w_in[..., o:]
    kr_blk = lambda x: jnp.concatenate([zeros(ROPE_OFF), x, zeros(HEAD_PAD - ROPE_OFF - ROPE_DIM)], axis=-1)
    rw_parts = [rw[..., :3 * D_RWKV]]
    for j in range(4):
        lo = 3 * D_RWKV + j * LORA_WA
        rw_parts += [rw[..., lo:lo + LORA_WA], zeros(LANE - LORA_WA)]
    rw_parts.append(rw[..., 3 * D_RWKV + 4 * LORA_WA:])
    perm = np.concatenate([h * RET_DK + _rope_perm(RET_DK) for h in range(2 * H_RET)])
    cols = [qa, kva, kr_blk(kr), kr_blk(kr[..., _rope_perm(ROPE_DIM)])] + rw_parts + [rt, rt[..., perm]]
    return jnp.concatenate(cols, axis=-1).astype(BF16)


def _layout_mu(mu):
    depth = mu.shape[0]
    parts = [mu[:, :3 * D_RWKV]]
    for j in range(4):
        lo = 3 * D_RWKV + j * LORA_WA
        parts += [mu[:, lo:lo + LORA_WA], jnp.zeros((depth, LANE - LORA_WA), mu.dtype)]
    parts.append(mu[:, 3 * D_RWKV + 4 * LORA_WA:])
    return jnp.concatenate(parts, axis=-1).reshape(depth, 1, RW_W)


def _layout_wq(wq):
    depth = wq.shape[0]
    w = wq.reshape(depth, Q_LORA, H_MLA, QK_NOPE + ROPE_DIM)
    pad = jnp.zeros((depth, Q_LORA, H_MLA, HEAD_PAD - QK_NOPE - ROPE_DIM), wq.dtype)
    nope, rope = w[..., :QK_NOPE], w[..., QK_NOPE:]
    main = jnp.concatenate([nope, rope, pad], axis=-1)
    swap = jnp.concatenate([jnp.zeros_like(nope), rope[..., _rope_perm(ROPE_DIM)], pad], axis=-1)
    shape = (depth, Q_LORA, H_MLA * HEAD_PAD)
    return main.reshape(shape).astype(BF16), swap.reshape(shape).astype(BF16)


def _layout_wkv(wkv):
    depth = wkv.shape[0]
    w = wkv.reshape(depth, KV_LORA, H_MLA, QK_NOPE + V_HD)
    kn = jnp.concatenate([w[..., :QK_NOPE], jnp.zeros((depth, KV_LORA, H_MLA, HEAD_PAD - QK_NOPE), wkv.dtype)], axis=-1)
    return (kn.reshape(depth, KV_LORA, H_MLA * HEAD_PAD).astype(BF16),
            w[..., QK_NOPE:].reshape(depth, KV_LORA, D_MLA).astype(BF16))


def _layout_lora_up(up):
    depth = up.shape[0]
    return jnp.concatenate([up, jnp.zeros((depth, 4, LANE - LORA_WA, D_RWKV), up.dtype)], axis=2).astype(BF16)


def _block_diag(s):
    eye = jnp.eye(H_RWKV, dtype=s.dtype)
    out = jnp.einsum('...hij,hg->...higj', s, eye)
    return out.reshape(s.shape[:-3] + (STACK, STACK))


def _unpack_state(s):
    nb = s.shape[0]
    return s.reshape(nb, 2, RWKV_HD, H_RWKV, RWKV_HD).transpose(0, 1, 3, 2, 4)


def kernel(x_prompt, x_sample, cache_mla_ckv, cache_mla_krope, state_rwkv, state_ret, c, c_ctx,
           norm_g, w_mod, b_mod, ffn_wi, ffn_wo, w_in, w_out, mla_norm_q, mla_norm_kv, mla_wq_up,
           mla_wkv_up, rwkv_mu, rwkv_vec, rwkv_lora_up, rwkv_g_up, ret_gn):
    bc, nc, _ = x_prompt.shape
    bl, nl, _ = x_sample.shape
    depth = norm_g.shape[0]
    past = cache_mla_ckv.shape[2]
    tc, tl = bc * nc, bl * nl
    assert bl <= 7 and nl % GRID_W == 0
    assert nc % CHUNK == 0 and nl % CHUNK == 0 and tc % nl == 0 and tc % past == 0
    assert bc % PAIR == 0 and bl % PAIR == 0

    def tile(cap):
        tm = cap
        while tc % tm or nl % tm:
            tm //= 2
        return tm

    def grouper(tm):
        ctx_tiles, per_seq = tc // tm, nl // tm
        return lambda i: jnp.where(i < ctx_tiles, 0, 1 + jnp.maximum(i - ctx_tiles, 0) // per_seq)

    tm_ffn, tm_tok = tile(1024), tile(512)
    tm_shift = 256
    while nc % tm_shift or nl % tm_shift:
        tm_shift //= 2
    tq_ctx, tq_lat = min(nc, 256), min(nl, 256)

    wi_b, wo_b = ffn_wi.astype(BF16), ffn_wo.astype(BF16)
    w_in_b = _layout_w_in(w_in)
    w_out_b = w_out.astype(BF16)
    wq_b, wqs_b = _layout_wq(mla_wq_up)
    wk_b, wv_b = _layout_wkv(mla_wkv_up)
    mu_l = _layout_mu(rwkv_mu)
    up_b = _layout_lora_up(rwkv_lora_up)
    gup_b = rwkv_g_up.astype(BF16)

    pos = jnp.arange(nl)
    row_l, col_l = pos // GRID_W, pos % GRID_W
    cos_m, sin_m = _rope_tables(row_l, col_l, ROPE_DIM)
    cos_r, sin_r = _rope_tables(row_l, col_l, RET_DK)

    def mla_table(tab, fill):
        lat = jnp.concatenate([jnp.full((nl, ROPE_OFF), fill, F32), tab,
                               jnp.zeros((nl, HEAD_PAD - ROPE_OFF - ROPE_DIM), F32)], axis=-1)
        ctx = jnp.concatenate([jnp.full((tc, ROPE_OFF + ROPE_DIM), fill, F32),
                               jnp.zeros((tc, HEAD_PAD - ROPE_OFF - ROPE_DIM), F32)], axis=-1)
        return jnp.concatenate([ctx, jnp.tile(lat, (bl, 1))], axis=0)

    def ret_table(tab, fill):
        return jnp.concatenate([jnp.full((tc, D_RET), fill, F32), jnp.tile(jnp.tile(tab, (1, H_RET)), (bl, 1))], axis=0)

    cq, sq = mla_table(cos_m, 1.0), mla_table(sin_m, 0.0)
    cr, sr = ret_table(cos_r, 1.0), ret_table(sin_r, 0.0)

    e = 5.0 + jnp.arange(H_RET, dtype=F32)[None, :] + 0.5 * jnp.arange(2, dtype=F32)[:, None]
    lg = jnp.log1p(-jnp.exp2(-e))
    lg_lane = jnp.repeat(lg, RET_DK, axis=1)
    lgm = jnp.broadcast_to(lg_lane[:, :, None], (2, STACK, STACK))
    lgl = lg_lane[:, None, :]

    c_all = jnp.concatenate([c_ctx[None, :], c, jnp.zeros((7 - bl, D_MODEL), F32)], axis=0)
    mod = _modulation(c_all, w_mod, b_mod).reshape(depth, 8, N_MOD, D_MODEL)

    kr_cache = jnp.pad(cache_mla_krope, ((0, 0), (0, 0), (0, 0), (ROPE_OFF, HEAD_PAD - ROPE_OFF - ROPE_DIM)))
    zero_state = jnp.zeros((bc, 2, STACK, STACK), F32)
    s0_rwkv_lat = _block_diag(state_rwkv.astype(F32))
    s0_ret_lat = _block_diag(state_ret.astype(F32))

    x = jnp.concatenate([x_prompt.reshape(tc, D_MODEL), x_sample.reshape(tl, D_MODEL)], axis=0)
    ckv_l, krope_l, rwkv_l, ret_l = [], [], [], []
    for l in range(depth):
        mod_l, ng = mod[l], norm_g[l]
        x = _ffn(x, mod_l, ng, wi_b, wo_b, l, 0, grouper(tm_ffn), tm_ffn)
        qa, kva, kr, krs, rw, rt, rts = _mixin(x, mod_l, ng, w_in_b[l], grouper(tm_tok), tm_tok)

        qh, ckv, kh, vv = _mla_prep(qa, kva, kr, krs, cq, sq, mla_norm_q[l][None, :], mla_norm_kv[l][None, :],
                                    wq_b[l], wqs_b[l], wk_b[l], wv_b[l], tm_tok)
        kh_c, v_c = _cache_kv(cache_mla_ckv[:, l].reshape(bl * past, KV_LORA),
                              kr_cache[:, l].reshape(bl * past, HEAD_PAD), wk_b[l], wv_b[l])
        mla_ctx = _attention(qh, kh, vv, None, None, nb=bc, n=nc, m1=nc, tok0=0, k1_tok0=0, tq=tq_ctx)
        mla_lat = _attention(qh, kh_c, v_c, kh, vv, nb=bl, n=nl, m1=past, tok0=tc, k1_tok0=0, tq=tq_lat)

        r, kk, v, lw, kka, kd, bonus, gate = _rwkv_prep(
            rw, mu_l[l], rwkv_vec[l], up_b[l], gup_b[l],
            tm=tm_shift, ctx_tiles=tc // tm_shift, ctx_tiles_per_seq=nc // tm_shift,
            lat_tiles_per_seq=nl // tm_shift)
        gn_r = rwkv_vec[l][7:8, :]
        yf_c, yb_c, s_rwkv = _rwkv_scan(r, kk, v, lw, kka, kd, zero_state, gn_r, nb=bc, n=nc, tok0=0)
        yf_l, yb_l, _ = _rwkv_scan(r, kk, v, lw, kka, kd, s0_rwkv_lat[:, l], gn_r, nb=bl, n=nl, tok0=tc)

        gn_t = ret_gn[l][:, None, :]
        of_c, ob_c, s_ret = _retention(rt, rts, cr, sr, lgm, lgl, gn_t, zero_state, nb=bc, n=nc, tok0=0)
        of_l, ob_l, _ = _retention(rt, rts, cr, sr, lgm, lgl, gn_t, s0_ret_lat[:, l], nb=bl, n=nl, tok0=tc)

        x = _mixout(x, mod_l, ng, bonus, gate, w_out_b[l], (mla_ctx, yf_c, yb_c, of_c, ob_c),
                    (mla_lat, yf_l, yb_l, of_l, ob_l), grouper(tm_tok), tm_tok, tc // tm_tok)
        x = _ffn(x, mod_l, ng, wi_b, wo_b, l, 1, grouper(tm_ffn), tm_ffn)

        ckv_l.append(ckv[:tc].reshape(bc, nc, KV_LORA))
        krope_l.append(kr[:tc, ROPE_OFF:ROPE_OFF + ROPE_DIM].reshape(bc, nc, ROPE_DIM))
        rwkv_l.append(_unpack_state(s_rwkv))
        ret_l.append(_unpack_state(s_ret))

    return (x[:tc].reshape(bc, nc, D_MODEL), x[tc:].reshape(bl, nl, D_MODEL),
            jnp.stack(ckv_l, axis=1), jnp.stack(krope_l, axis=1),
            jnp.stack(rwkv_l, axis=1), jnp.stack(ret_l, axis=1))
```

```python
import functools

import numpy as np
import jax
import jax.numpy as jnp
from jax import lax
from jax.experimental import pallas as pl
from jax.experimental.pallas import tpu as pltpu

F32 = jnp.float32
BF16 = jnp.bfloat16

D_MODEL = 1024
N_MOD = 9
D_FF = 2816
H_MLA = 8
QK_NOPE = 64
ROPE_DIM = 32
V_HD = 64
Q_LORA = 256
KV_LORA = 128
D_MLA = H_MLA * V_HD
MLA_SCALE = (QK_NOPE + ROPE_DIM) ** -0.5
LOG2E = 1.4426950408889634
H_RWKV = 4
RWKV_HD = 64
D_RWKV = H_RWKV * RWKV_HD
LORA_WA = 64
LORA_G = 128
DECAY_SCALE = 0.6065306597126334
RWKV_GN_EPS = 64e-5
H_RET = 4
RET_DK = 64
D_RET = H_RET * 64
RET_GN_EPS = 1e-5
ROPE_BASE = 10000.0
NORM_EPS = 1e-6
GRID_W = 64

LANE = 128
HEAD_PAD = 128
ROPE_OFF = QK_NOPE
CHUNK = 64
STACK = H_RWKV * CHUNK
PAIR = 4
ATTN_HEADS = 4
RWKV_GROUP = 2
GSTACK = RWKV_GROUP * CHUNK
VMEM_LIMIT = 56 * 1024 * 1024

C_QA = 0
C_KVA = C_QA + Q_LORA
C_KR = C_KVA + KV_LORA
C_KRS = C_KR + LANE
C_RW = C_KRS + LANE
RW_W = 3 * D_RWKV + 4 * LANE + LORA_G
C_RT = C_RW + RW_W
RT_W = 2 * H_RET * RET_DK + 3 * D_RET
C_RTS = C_RT + RT_W
RTS_W = 2 * H_RET * RET_DK
IN_W = C_RTS + RTS_W


def _params(*sem):
    return pltpu.CompilerParams(dimension_semantics=sem, vmem_limit_bytes=VMEM_LIMIT)


def _sigmoid(x):
    return 1.0 / (1.0 + jnp.exp(-x))


def _rms(x, g):
    return x * lax.rsqrt(jnp.mean(x * x, axis=-1, keepdims=True) + NORM_EPS) * g


def _bdot(a, b):
    return jnp.dot(a.astype(BF16), b.astype(BF16), preferred_element_type=F32)


def _bdot_nt(a, b):
    return lax.dot_general(a.astype(BF16), b.astype(BF16), (((1,), (1,)), ((), ())),
                           preferred_element_type=F32)


def _bdot_tn(a, b):
    return lax.dot_general(a.astype(BF16), b.astype(BF16), (((0,), (0,)), ((), ())),
                           preferred_element_type=F32)


def _split_bf16(x):
    hi = x.astype(BF16)
    return hi, (x - hi.astype(F32)).astype(BF16)


def _dot_exact_rhs(x, m):
    hi, lo = _split_bf16(x)
    mb = m.astype(BF16)
    return jnp.dot(hi, mb, preferred_element_type=F32) + jnp.dot(lo, mb, preferred_element_type=F32)


def _dot_exact_lhs(m, x):
    hi, lo = _split_bf16(x)
    mb = m.astype(BF16)
    return jnp.dot(mb, hi, preferred_element_type=F32) + jnp.dot(mb, lo, preferred_element_type=F32)


def _head_mean_matrix():
    r = lax.broadcasted_iota(jnp.int32, (STACK, STACK), 0) // RWKV_HD
    c = lax.broadcasted_iota(jnp.int32, (STACK, STACK), 1) // RWKV_HD
    return jnp.where(r == c, 1.0 / RWKV_HD, 0.0).astype(F32)


def _head_groupnorm(y, gain, eps):
    mm = _head_mean_matrix()
    yc = y - _dot_exact_rhs(y, mm)
    var = _dot_exact_rhs(yc * yc, mm)
    return yc * lax.rsqrt(var + eps) * gain


def _stack_heads(x):
    head = lax.broadcasted_iota(jnp.int32, x.shape, 1) // RWKV_HD
    return jnp.concatenate([jnp.where(head == h, x, 0.0) for h in range(H_RWKV)], axis=0)


def _unstack_heads(x):
    out = x[0:CHUNK]
    for h in range(1, H_RWKV):
        out = out + x[h * CHUNK:(h + 1) * CHUNK]
    return out


def _stack_group(x, grp):
    head = lax.broadcasted_iota(jnp.int32, x.shape, 1) // RWKV_HD
    return jnp.concatenate([jnp.where(head == grp * RWKV_GROUP + j, x, 0.0) for j in range(RWKV_GROUP)], axis=0)


def _unstack_group(x):
    out = x[0:CHUNK]
    for j in range(1, RWKV_GROUP):
        out = out + x[j * CHUNK:(j + 1) * CHUNK]
    return out


def _mod_body(c_ref, w_ref, b_ref, o_ref):
    c = c_ref[...]
    s = c * _sigmoid(c)
    o_ref[0] = _bdot(s, w_ref[0]) + b_ref[0]


def _modulation(c_all, w_mod, b_mod):
    depth, d, nd = w_mod.shape
    tn = D_MODEL
    return pl.pallas_call(
        _mod_body,
        grid=(depth, nd // tn),
        in_specs=[pl.BlockSpec((8, d), lambda l, j: (0, 0)),
                  pl.BlockSpec((1, d, tn), lambda l, j: (l, 0, j)),
                  pl.BlockSpec((1, 1, tn), lambda l, j: (l, 0, j))],
        out_specs=pl.BlockSpec((1, 8, tn), lambda l, j: (l, 0, j)),
        out_shape=jax.ShapeDtypeStruct((depth, 8, nd), F32),
        compiler_params=_params("parallel", "parallel"),
        name="modulation",
    )(c_all, w_mod, b_mod.reshape(depth, 1, nd))


def _ffn_body(x_ref, mod_ref, ng_ref, wg_ref, wu_ref, wo_ref, o_ref, h_ref, acc_ref, *, mrow, grow):
    f = pl.program_id(1)

    @pl.when(f == 0)
    def _():
        m = mod_ref[0]
        h = _rms(x_ref[...], ng_ref[grow:grow + 1, :]) * (1.0 + m[mrow + 1:mrow + 2, :]) + m[mrow:mrow + 1, :]
        h_ref[...] = h.astype(BF16)
        acc_ref[...] = jnp.zeros_like(acc_ref)

    h = h_ref[...]
    g = jnp.dot(h, wg_ref[...], preferred_element_type=F32)
    u = jnp.dot(h, wu_ref[...], preferred_element_type=F32)
    a = (g * _sigmoid(g)) * u
    acc_ref[...] += jnp.dot(a.astype(BF16), wo_ref[...], preferred_element_type=F32)

    @pl.when(f == pl.num_programs(1) - 1)
    def _():
        m = mod_ref[0]
        o_ref[...] = x_ref[...] + 0.5 * m[mrow + 2:mrow + 3, :] * _rms(acc_ref[...], ng_ref[grow + 1:grow + 2, :])


def _ffn(x, mod, ng, wi, wo, layer, which, grp, tm):
    t = x.shape[0]
    tf = 256
    nf = D_FF // tf
    body = functools.partial(_ffn_body, mrow=6 * which, grow=4 * which)
    return pl.pallas_call(
        body,
        grid=(t // tm, nf),
        in_specs=[pl.BlockSpec((tm, D_MODEL), lambda i, f: (i, 0)),
                  pl.BlockSpec((1, N_MOD, D_MODEL), lambda i, f: (grp(i), 0, 0)),
                  pl.BlockSpec((6, D_MODEL), lambda i, f: (0, 0)),
                  pl.BlockSpec((None, None, D_MODEL, tf), lambda i, f: (layer, which, 0, f)),
                  pl.BlockSpec((None, None, D_MODEL, tf), lambda i, f: (layer, which, 0, f + nf)),
                  pl.BlockSpec((None, None, tf, D_MODEL), lambda i, f: (layer, which, f, 0))],
        out_specs=pl.BlockSpec((tm, D_MODEL), lambda i, f: (i, 0)),
        out_shape=jax.ShapeDtypeStruct((t, D_MODEL), F32),
        scratch_shapes=[pltpu.VMEM((tm, D_MODEL), BF16), pltpu.VMEM((tm, D_MODEL), F32)],
        compiler_params=_params("parallel", "arbitrary"),
        name="ffn",
    )(x, mod, ng, wi, wi, wo)


def _mixin_body(x_ref, mod_ref, ng_ref, w_ref, qa_ref, kva_ref, kr_ref, krs_ref, rw_ref, rt_ref, rts_ref):
    m = mod_ref[0]
    h = (_rms(x_ref[...], ng_ref[2:3, :]) * (1.0 + m[4:5, :]) + m[3:4, :]).astype(BF16)

    def proj(lo, width):
        return jnp.dot(h, w_ref[:, lo:lo + width], preferred_element_type=F32)

    qa_ref[...] = proj(C_QA, Q_LORA)
    kva_ref[...] = proj(C_KVA, KV_LORA)
    kr_ref[...] = proj(C_KR, LANE)
    krs_ref[...] = proj(C_KRS, LANE)
    rw_ref[...] = proj(C_RW, RW_W)
    rt_ref[...] = proj(C_RT, RT_W)
    rts_ref[...] = proj(C_RTS, RTS_W)


def _mixin(x, mod, ng, w_in, grp, tm):
    t = x.shape[0]
    widths = (Q_LORA, KV_LORA, LANE, LANE, RW_W, RT_W, RTS_W)
    return pl.pallas_call(
        _mixin_body,
        grid=(t // tm,),
        in_specs=[pl.BlockSpec((tm, D_MODEL), lambda i: (i, 0)),
                  pl.BlockSpec((1, N_MOD, D_MODEL), lambda i: (grp(i), 0, 0)),
                  pl.BlockSpec((6, D_MODEL), lambda i: (0, 0)),
                  pl.BlockSpec((D_MODEL, IN_W), lambda i: (0, 0))],
        out_specs=[pl.BlockSpec((tm, w), lambda i: (i, 0)) for w in widths],
        out_shape=[jax.ShapeDtypeStruct((t, w), F32) for w in widths],
        compiler_params=_params("parallel"),
        name="mixer_in",
    )(x, mod, ng, w_in)


def _kv_expand(ckv, krr, wk_ref, wv_ref, kh_ref, v_ref):
    c = ckv.astype(BF16)
    kn = jnp.dot(c, wk_ref[...], preferred_element_type=F32)
    v = jnp.dot(c, wv_ref[...], preferred_element_type=F32)
    for h in range(H_MLA):
        kh_ref[h] = (kn[:, h * HEAD_PAD:(h + 1) * HEAD_PAD] + krr).astype(BF16)
    for p in range(H_MLA // 2):
        v_ref[p] = v[:, p * LANE:(p + 1) * LANE].astype(BF16)


def _mla_prep_body(qa_ref, kva_ref, kr_ref, krs_ref, cq_ref, sq_ref, gq_ref, gkv_ref, wq_ref, wqs_ref,
                   wk_ref, wv_ref, qh_ref, ckv_ref, kh_ref, v_ref):
    cq = cq_ref[...]
    sq = sq_ref[...]
    qn = _rms(qa_ref[...], gq_ref[...]).astype(BF16)
    q = jnp.dot(qn, wq_ref[...], preferred_element_type=F32)
    qs = jnp.dot(qn, wqs_ref[...], preferred_element_type=F32)
    for h in range(H_MLA):
        sl = slice(h * HEAD_PAD, (h + 1) * HEAD_PAD)
        qh_ref[h] = (q[:, sl] * cq + qs[:, sl] * sq).astype(BF16)
    ckv = _rms(kva_ref[...], gkv_ref[...])
    ckv_ref[...] = ckv
    krr = kr_ref[...] * cq + krs_ref[...] * sq
    _kv_expand(ckv, krr, wk_ref, wv_ref, kh_ref, v_ref)


def _mla_prep(qa, kva, kr, krs, cq, sq, gq, gkv, wq, wqs, wk, wv, tm):
    t = qa.shape[0]
    row = lambda w: pl.BlockSpec((tm, w), lambda i: (i, 0))
    full = lambda a: pl.BlockSpec(a.shape, lambda i: (0,) * a.ndim)
    return pl.pallas_call(
        _mla_prep_body,
        grid=(t // tm,),
        in_specs=[row(Q_LORA), row(KV_LORA), row(LANE), row(LANE), row(LANE), row(LANE),
                  full(gq), full(gkv), full(wq), full(wqs), full(wk), full(wv)],
        out_specs=[pl.BlockSpec((H_MLA, tm, HEAD_PAD), lambda i: (0, i, 0)),
                   row(KV_LORA),
                   pl.BlockSpec((H_MLA, tm, HEAD_PAD), lambda i: (0, i, 0)),
                   pl.BlockSpec((H_MLA // 2, tm, LANE), lambda i: (0, i, 0))],
        out_shape=[jax.ShapeDtypeStruct((H_MLA, t, HEAD_PAD), BF16),
                   jax.ShapeDtypeStruct((t, KV_LORA), F32),
                   jax.ShapeDtypeStruct((H_MLA, t, HEAD_PAD), BF16),
                   jax.ShapeDtypeStruct((H_MLA // 2, t, LANE), BF16)],
        compiler_params=_params("parallel"),
        name="mla_prep",
    )(qa, kva, kr, krs, cq, sq, gq, gkv, wq, wqs, wk, wv)


def _cache_kv_body(ckv_ref, kr_ref, wk_ref, wv_ref, kh_ref, v_ref):
    _kv_expand(ckv_ref[...], kr_ref[...], wk_ref, wv_ref, kh_ref, v_ref)


def _cache_kv(ckv, kr, wk, wv):
    t = ckv.shape[0]
    tm = min(t, 512)
    row = lambda w: pl.BlockSpec((tm, w), lambda i: (i, 0))
    full = lambda a: pl.BlockSpec(a.shape, lambda i: (0,) * a.ndim)
    return pl.pallas_call(
        _cache_kv_body,
        grid=(t // tm,),
        in_specs=[row(KV_LORA), row(LANE), full(wk), full(wv)],
        out_specs=[pl.BlockSpec((H_MLA, tm, HEAD_PAD), lambda i: (0, i, 0)),
                   pl.BlockSpec((H_MLA // 2, tm, LANE), lambda i: (0, i, 0))],
        out_shape=[jax.ShapeDtypeStruct((H_MLA, t, HEAD_PAD), BF16),
                   jax.ShapeDtypeStruct((H_MLA // 2, t, LANE), BF16)],
        compiler_params=_params("parallel"),
        name="cache_kv",
    )(ckv, kr, wk, wv)


def _attn_body(*refs, two):
    if two:
        q_ref, k1_ref, v1_ref, k2_ref, v2_ref, o_ref = refs
    else:
        q_ref, k1_ref, v1_ref, o_ref = refs
    nt = (((1,), (1,)), ((), ()))
    krefs = (k1_ref, k2_ref) if two else (k1_ref,)
    vrefs = (v1_ref, v2_ref) if two else (v1_ref,)

    def scores(j):
        return [lax.dot_general(q_ref[j], k[j], nt, preferred_element_type=F32) for k in krefs]

    def softmax(ss):
        m = functools.reduce(jnp.maximum, [jnp.max(s, axis=-1, keepdims=True) for s in ss])
        ps = [jnp.exp2((s - m) * (MLA_SCALE * LOG2E)) for s in ss]
        return [p.astype(BF16) for p in ps], sum(jnp.sum(p, axis=-1, keepdims=True) for p in ps)

    def weighted(j, ps, l):
        return sum(jnp.dot(p, v[j // 2], preferred_element_type=F32) for p, v in zip(ps, vrefs)) / l

    outs, ss, pl_ = [], {}, {}
    for step in range(ATTN_HEADS + 2):
        if step < ATTN_HEADS:
            ss[step] = scores(step)
        if 1 <= step <= ATTN_HEADS:
            pl_[step - 1] = softmax(ss.pop(step - 1))
        if step >= 2:
            outs.append(weighted(step - 2, *pl_.pop(step - 2)))
    lane = lax.broadcasted_iota(jnp.int32, outs[0].shape, 1)
    for p in range(ATTN_HEADS // 2):
        o_ref[:, p * LANE:(p + 1) * LANE] = jnp.where(lane < V_HD, outs[2 * p], outs[2 * p + 1])


def _attention(qh, kh1, v1, kh2, v2, *, nb, n, m1, tok0, k1_tok0, tq):
    two = kh2 is not None
    nq = n // tq
    qb0 = tok0 // tq
    hp, vp = ATTN_HEADS, ATTN_HEADS // 2
    in_specs = [pl.BlockSpec((hp, tq, HEAD_PAD), lambda b, p, i: (p, qb0 + b * nq + i, 0)),
                pl.BlockSpec((hp, m1, HEAD_PAD), lambda b, p, i: (p, k1_tok0 // m1 + b, 0)),
                pl.BlockSpec((vp, m1, LANE), lambda b, p, i: (p, k1_tok0 // m1 + b, 0))]
    args = [qh, kh1, v1]
    if two:
        in_specs += [pl.BlockSpec((hp, n, HEAD_PAD), lambda b, p, i: (p, tok0 // n + b, 0)),
                     pl.BlockSpec((vp, n, LANE), lambda b, p, i: (p, tok0 // n + b, 0))]
        args += [kh2, v2]
    return pl.pallas_call(
        functools.partial(_attn_body, two=two),
        grid=(nb, H_MLA // hp, nq),
        in_specs=in_specs,
        out_specs=pl.BlockSpec((tq, vp * LANE), lambda b, p, i: (b * nq + i, p)),
        out_shape=jax.ShapeDtypeStruct((nb * n, D_MLA), F32),
        compiler_params=_params("parallel", "parallel", "parallel"),
        name="mla_attention",
    )(*args)


def _rwkv_prep_body(x_ref, xp_ref, xn_ref, mu_ref, vec_ref, up_ref, gup_ref,
                    r_ref, kk_ref, v_ref, lw_ref, kka_ref, kd_ref, bonus_ref, gate_ref,
                    *, tm, ctx_tiles, ctx_tiles_per_seq, lat_tiles_per_seq):
    i = pl.program_id(0)
    in_ctx = i < ctx_tiles
    per_seq = jnp.where(in_ctx, ctx_tiles_per_seq, lat_tiles_per_seq)
    j = jnp.where(in_ctx, i % ctx_tiles_per_seq, jnp.maximum(i - ctx_tiles, 0) % lat_tiles_per_seq)
    first = j == 0
    last = j == per_seq - 1
    x = x_ref[...]
    rowi = lax.broadcasted_iota(jnp.int32, x.shape, 0)
    prev_row = jnp.where(first, 0.0, xp_ref[7:8, :])
    next_row = jnp.where(last, 0.0, xn_ref[0:1, :])
    xprev = jnp.where(rowi == 0, prev_row, pltpu.roll(x, 1, 0))
    xnext = jnp.where(rowi == tm - 1, next_row, pltpu.roll(x, tm - 1, 0))
    xr = x + mu_ref[...] * (0.5 * (xprev + xnext) - x)

    vec = vec_ref[...]
    r = xr[:, 0:D_RWKV]
    kr = xr[:, D_RWKV:2 * D_RWKV]
    vr = xr[:, 2 * D_RWKV:3 * D_RWKV]
    o = 3 * D_RWKV
    zw = (xr[:, o:o + LANE], xr[:, o + LANE:o + 2 * LANE])
    za = (xr[:, o + 2 * LANE:o + 3 * LANE], xr[:, o + 3 * LANE:o + 4 * LANE])
    zg = xr[:, o + 4 * LANE:o + 5 * LANE]

    mm = _head_mean_matrix() * float(RWKV_HD)
    kk = kr * vec[4:5, :]
    kk = kk / (jnp.sqrt(_dot_exact_rhs(kk * kk, mm)) + 1e-12)
    r_ref[...] = r
    kk_ref[...] = kk
    v_ref[...] = vr
    gate_ref[...] = _bdot(_sigmoid(zg), gup_ref[...])
    bonus = jnp.zeros_like(r)
    for z in range(2):
        zwz = _bdot(jnp.tanh(zw[z]), up_ref[z]) + vec[z:z + 1, :]
        lw_ref[z] = -DECAY_SCALE * _sigmoid(zwz)
        a = _sigmoid(_bdot(za[z], up_ref[2 + z]) + vec[2 + z:3 + z, :])
        kd = kr * (1.0 + (a - 1.0) * vec[5:6, :])
        kka_ref[z] = kk * a
        kd_ref[z] = kd
        bonus = bonus + _dot_exact_rhs(r * kd * vec[6:7, :], mm) * vr
    bonus_ref[...] = bonus


def _rwkv_prep(rw, mu, vec, up, gup, *, tm, ctx_tiles, ctx_tiles_per_seq, lat_tiles_per_seq):
    t = rw.shape[0]
    hb = tm // 8
    nhb = t // 8
    row = lambda: pl.BlockSpec((tm, D_RWKV), lambda i: (i, 0))
    two = lambda: pl.BlockSpec((2, tm, D_RWKV), lambda i: (0, i, 0))
    full = lambda a: pl.BlockSpec(a.shape, lambda i: (0,) * a.ndim)
    body = functools.partial(_rwkv_prep_body, tm=tm, ctx_tiles=ctx_tiles, ctx_tiles_per_seq=ctx_tiles_per_seq,
                             lat_tiles_per_seq=lat_tiles_per_seq)
    one = jax.ShapeDtypeStruct((t, D_RWKV), F32)
    both = jax.ShapeDtypeStruct((2, t, D_RWKV), F32)
    return pl.pallas_call(
        body,
        grid=(t // tm,),
        in_specs=[pl.BlockSpec((tm, RW_W), lambda i: (i, 0)),
                  pl.BlockSpec((8, RW_W), lambda i: (jnp.maximum(i * hb - 1, 0), 0)),
                  pl.BlockSpec((8, RW_W), lambda i: (jnp.minimum((i + 1) * hb, nhb - 1), 0)),
                  full(mu), full(vec), full(up), full(gup)],
        out_specs=[row(), row(), row(), two(), two(), two(), row(), row()],
        out_shape=[one, one, one, both, both, both, one, one],
        compiler_params=_params("parallel"),
        name="rwkv_prep",
    )(rw, rw, rw, mu, vec, up, gup)


def _unit_lower_inverse(lmats, rowi, coli):
    eye = jnp.where(rowi == coli, 1.0, 0.0).astype(F32)
    n1 = [jnp.where((rowi >> 3) == (coli >> 3), lm, 0.0) for lm in lmats]
    n2 = [_bdot(x, x) for x in n1]
    n4 = [_bdot(x, x) for x in n2]
    t = [_bdot(eye - x, eye + y) for x, y in zip(n1, n2)]
    t = [_bdot(x, eye + y) for x, y in zip(t, n4)]
    for sh in (4, 5, 6):
        inner = (rowi >> (sh - 1)) == (coli >> (sh - 1))
        outer = (rowi >> sh) == (coli >> sh)
        off = jnp.logical_and(outer, jnp.logical_not(inner))
        tb = [x.astype(BF16) for x in t]
        et = [_bdot(jnp.where(off, lm, 0.0), x) for lm, x in zip(lmats, tb)]
        t = [x - _bdot(xb, y) for x, xb, y in zip(t, tb, et)]
    return t


def _scan_masks(z):
    sgn = 1 - 2 * z
    rowi = lax.broadcasted_iota(jnp.int32, (GSTACK, GSTACK), 0)
    coli = lax.broadcasted_iota(jnp.int32, (GSTACK, GSTACK), 1)
    strict = ((coli & (CHUNK - 1)) - (rowi & (CHUNK - 1))) * sgn < 0
    row2 = lax.broadcasted_iota(jnp.int32, (GSTACK, 2 * GSTACK), 0)
    col2 = lax.broadcasted_iota(jnp.int32, (GSTACK, 2 * GSTACK), 1)
    incl2 = ((col2 & (CHUNK - 1)) - (row2 & (CHUNK - 1))) * sgn <= 0
    ri = lax.broadcasted_iota(jnp.int32, (CHUNK, CHUNK), 0)
    ci = lax.broadcasted_iota(jnp.int32, (CHUNK, CHUNK), 1)
    cmat = jnp.where((ci - ri) * sgn <= 0, 1.0, 0.0).astype(F32)
    return strict, incl2, cmat


def _rwkv_steps(chains):
    gs = GSTACK
    rowi = lax.broadcasted_iota(jnp.int32, (gs, gs), 0)
    coli = lax.broadcasted_iota(jnp.int32, (gs, gs), 1)
    groups = range(H_RWKV // RWKV_GROUP)
    aq, bc, vs, wtot, owner = [], [], [], [], []
    for ci, (r, kk, v, lw, kka, kd, s, (strict, incl2, cmat)) in enumerate(chains):
        cum = _dot_exact_lhs(cmat, lw)
        iw = jnp.exp(-cum)
        a = jnp.exp(cum - lw) * kk
        q = r * jnp.exp(cum)
        b, cm = kka * iw, kd * iw
        wtot.append(jnp.exp(jnp.sum(lw, axis=0, keepdims=True)))
        for grp in groups:
            aq.append(jnp.concatenate([_stack_group(a, grp), _stack_group(q, grp)], axis=0).astype(BF16))
            bc.append(jnp.concatenate([_stack_group(b, grp), _stack_group(cm, grp)], axis=0).astype(BF16))
            vs.append(_stack_group(v, grp).astype(BF16))
            owner.append(chains[ci])
    g = [_bdot_nt(x, y) for x, y in zip(aq, bc)]
    aqs = [_bdot_nt(x, ch[6]) for x, ch in zip(aq, owner)]
    lab = [jnp.where(ch[7][0], x[0:gs, 0:gs], 0.0) for x, ch in zip(g, owner)]
    lac = [jnp.where(ch[7][0], x[0:gs, gs:], 0.0) for x, ch in zip(g, owner)]
    rhs = [x[0:gs] + _bdot(y, v) for x, y, v in zip(aqs, lac, vs)]
    tinv = _unit_lower_inverse(lab, rowi, coli)
    uv = [jnp.concatenate([(-_bdot(t, x)).astype(BF16), v], axis=0) for t, x, v in zip(tinv, rhs, vs)]
    ys = [ax[gs:] + _bdot(jnp.where(ch[7][1], gx[gs:, :], 0.0), uvx) for ch, gx, ax, uvx in zip(owner, g, aqs, uv)]
    ds = [_bdot_tn(uvx, bcx) for uvx, bcx in zip(uv, bc)]
    out, ng = [], len(groups)
    for ci, ch in enumerate(chains):
        y = sum(_unstack_group(t) for t in ys[ci * ng:(ci + 1) * ng])
        s_new = (ch[6] + sum(ds[ci * ng:(ci + 1) * ng])) * wtot[ci]
        out.append((y, s_new))
    return out


def _rwkv_scan_body(*refs):
    n_in = 6 * 2 * PAIR
    s0_ref, gn_ref, yf_ref, yb_ref, sfin_ref, s_ref = refs[n_in:]
    c = pl.program_id(1)

    @pl.when(c == 0)
    def _():
        s_ref[...] = s0_ref[...]

    masks = [_scan_masks(z) for z in range(2)]
    ids = [(bb, z) for bb in range(PAIR) for z in range(2)]
    chains = []
    for bb, z in ids:
        r_ref, kk_ref, v_ref, lw_ref, kka_ref, kd_ref = refs[6 * (2 * bb + z):6 * (2 * bb + z) + 6]
        chains.append((r_ref[...], kk_ref[...], v_ref[...], lw_ref[0], kka_ref[0], kd_ref[0], s_ref[bb, z], masks[z]))
    results = _rwkv_steps(chains)
    gn = gn_ref[...]
    for (bb, z), (y, s_new) in zip(ids, results):
        s_ref[bb, z] = s_new
        (yf_ref, yb_ref)[z][0, bb] = _head_groupnorm(y, gn, RWKV_GN_EPS)

    @pl.when(c == pl.num_programs(1) - 1)
    def _():
        for (bb, z), (_, s_new) in zip(ids, results):
            sfin_ref[bb, z] = _unstack_heads(s_new)


def _scan_specs(nb, n, tok0):
    nc = n // CHUNK
    cb0 = tok0 // CHUNK

    def cidx(bb, z):
        return lambda i, c: cb0 + (PAIR * i + bb) * nc + c + z * (nc - 1 - 2 * c)

    state = pl.BlockSpec((PAIR, 2, STACK, STACK), lambda i, c: (i, 0, 0, 0))
    out_specs = [pl.BlockSpec((1, PAIR, CHUNK, D_RWKV), lambda i, c: (i, 0, c, 0)),
                 pl.BlockSpec((1, PAIR, CHUNK, D_RWKV), lambda i, c: (i, 0, nc - 1 - c, 0)),
                 pl.BlockSpec((PAIR, 2, RWKV_HD, STACK), lambda i, c: (i, 0, 0, 0))]
    out_shape = [jax.ShapeDtypeStruct((nb // PAIR, PAIR, n, D_RWKV), F32),
                 jax.ShapeDtypeStruct((nb // PAIR, PAIR, n, D_RWKV), F32),
                 jax.ShapeDtypeStruct((nb, 2, RWKV_HD, STACK), F32)]
    return nc, cidx, state, out_specs, out_shape


def _rwkv_scan(r, kk, v, lw, kka, kd, s0, gn, *, nb, n, tok0):
    nc, cidx, state, out_specs, out_shape = _scan_specs(nb, n, tok0)
    in_specs, args = [], []
    for bb in range(PAIR):
        for z in range(2):
            ci = cidx(bb, z)
            for arr in (r, kk, v):
                in_specs.append(pl.BlockSpec((CHUNK, D_RWKV), lambda i, c, ci=ci: (ci(i, c), 0)))
                args.append(arr)
            for arr in (lw, kka, kd):
                in_specs.append(pl.BlockSpec((1, CHUNK, D_RWKV), lambda i, c, ci=ci, z=z: (z, ci(i, c), 0)))
                args.append(arr)
    in_specs += [state, pl.BlockSpec((1, D_RWKV), lambda i, c: (0, 0))]
    y_f, y_b, sfin = pl.pallas_call(
        _rwkv_scan_body,
        grid=(nb // PAIR, nc),
        in_specs=in_specs,
        out_specs=out_specs,
        out_shape=out_shape,
        scratch_shapes=[pltpu.VMEM((PAIR, 2, STACK, STACK), F32)],
        compiler_params=_params("parallel", "arbitrary"),
        name="rwkv_scan",
    )(*args, s0, gn)
    return y_f.reshape(nb * n, D_RWKV), y_b.reshape(nb * n, D_RWKV), sfin


def _ret_consts(z, lgm, lgl):
    sgn = 1 - 2 * z
    ti = lax.broadcasted_iota(jnp.int32, (CHUNK, D_RET), 0)
    pos = (ti if z == 0 else CHUNK - 1 - ti).astype(F32)
    xi = jnp.exp(lgl * (pos + 1.0))
    zeta = jnp.exp(lgl * (CHUNK - 1.0 - pos))
    rowi = lax.broadcasted_iota(jnp.int32, (STACK, STACK), 0)
    coli = lax.broadcasted_iota(jnp.int32, (STACK, STACK), 1)
    dist = ((rowi & (CHUNK - 1)) - (coli & (CHUNK - 1))) * sgn
    dmask = jnp.where(dist >= 0, jnp.exp(lgm * jnp.maximum(dist, 0).astype(F32)), 0.0)
    return dmask, xi, zeta, jnp.exp(lgl * float(CHUNK))


def _ret_steps(chains):
    qr = [q * cr + qs * sr for q, k, v, g, qs, ks, cr, sr, s, cst, gn in chains]
    kr = [(k * cr + ks * sr) * (RET_DK ** -0.5) for q, k, v, g, qs, ks, cr, sr, s, cst, gn in chains]
    vs = [_stack_heads(ch[2]).astype(BF16) for ch in chains]
    scores = [_bdot_nt(_stack_heads(q), _stack_heads(k)) * ch[9][0] for q, k, ch in zip(qr, kr, chains)]
    cross = [_bdot(q * ch[9][1], ch[8]) for q, ch in zip(qr, chains)]
    inner = [_bdot(sc, v) for sc, v in zip(scores, vs)]
    kv = [_bdot_tn(_stack_heads(k * ch[9][2]), v) for k, v, ch in zip(kr, vs, chains)]
    out = []
    for ch, inn, crs, kvx in zip(chains, inner, cross, kv):
        g = ch[3]
        o = _head_groupnorm(_unstack_heads(inn) + crs, ch[10], RET_GN_EPS) * (g * _sigmoid(g))
        out.append((o, ch[8] * ch[9][3] + kvx))
    return out


def _ret_body(*refs):
    n_in = 8 * 2 * PAIR
    lgm_ref, lgl_ref, gn_ref, s0_ref, of_ref, ob_ref, sfin_ref, s_ref = refs[n_in:]
    c = pl.program_id(1)

    @pl.when(c == 0)
    def _():
        s_ref[...] = s0_ref[...]

    consts = [_ret_consts(z, lgm_ref[z], lgl_ref[z]) for z in range(2)]
    ids = [(bb, z) for bb in range(PAIR) for z in range(2)]
    chains = []
    for bb, z in ids:
        ins = [ref[...] for ref in refs[8 * (2 * bb + z):8 * (2 * bb + z) + 8]]
        chains.append((*ins, s_ref[bb, z], consts[z], gn_ref[z]))
    results = _ret_steps(chains)
    for (bb, z), (o, s_new) in zip(ids, results):
        s_ref[bb, z] = s_new
        (of_ref, ob_ref)[z][0, bb] = o

    @pl.when(c == pl.num_programs(1) - 1)
    def _():
        for (bb, z), (_, s_new) in zip(ids, results):
            sfin_ref[bb, z] = _unstack_heads(s_new)


def _retention(rt, rts, cr, sr, lgm, lgl, gn, s0, *, nb, n, tok0):
    nc, cidx, state, out_specs, out_shape = _scan_specs(nb, n, tok0)
    in_specs, args = [], []
    for bb in range(PAIR):
        for z in range(2):
            ci = cidx(bb, z)
            for arr, col in ((rt, 0), (rt, 1), (rt, 2), (rt, 3 + z), (rts, 0), (rts, 1), (cr, 0), (sr, 0)):
                in_specs.append(pl.BlockSpec((CHUNK, D_RET), lambda i, c, ci=ci, col=col: (ci(i, c), col)))
                args.append(arr)
    full = lambda a: pl.BlockSpec(a.shape, lambda i, c: (0,) * a.ndim)
    in_specs += [full(lgm), full(lgl), full(gn), state]
    o_f, o_b, sfin = pl.pallas_call(
        _ret_body,
        grid=(nb // PAIR, nc),
        in_specs=in_specs,
        out_specs=out_specs,
        out_shape=out_shape,
        scratch_shapes=[pltpu.VMEM((PAIR, 2, STACK, STACK), F32)],
        compiler_params=_params("parallel", "arbitrary"),
        name="retention",
    )(*args, lgm, lgl, gn, s0)
    return o_f.reshape(nb * n, D_RET), o_b.reshape(nb * n, D_RET), sfin


def _mixout_body(x_ref, mod_ref, ng_ref, bonus_ref, gate_ref, w_ref, *refs, ctx_tiles):
    ctx_refs, lat_refs, o_ref = refs[0:5], refs[5:10], refs[10]
    in_ctx = pl.program_id(0) < ctx_tiles
    mla, yf, yb, of, ob = [jnp.where(in_ctx, a[...], b[...]) for a, b in zip(ctx_refs, lat_refs)]
    m = mod_ref[0]
    rwkv_o = (yf + yb + bonus_ref[...]) * gate_ref[...]
    ret_o = of + ob
    mixed = (_bdot(mla, w_ref[0:D_MLA, :])
             + _bdot(rwkv_o, w_ref[D_MLA:D_MLA + D_RWKV, :])
             + _bdot(ret_o, w_ref[D_MLA + D_RWKV:, :]))
    o_ref[...] = x_ref[...] + m[5:6, :] * _rms(mixed, ng_ref[3:4, :])


def _mixout(x, mod, ng, bonus, gate, w_out, ctx_parts, lat_parts, grp, tm, ctx_tiles):
    t = x.shape[0]
    row = lambda w: pl.BlockSpec((tm, w), lambda i: (i, 0))
    ctx = lambda a: pl.BlockSpec((tm, a.shape[1]), lambda i: (jnp.minimum(i, ctx_tiles - 1), 0))
    lat = lambda a: pl.BlockSpec((tm, a.shape[1]), lambda i: (jnp.maximum(i - ctx_tiles, 0), 0))
    return pl.pallas_call(
        functools.partial(_mixout_body, ctx_tiles=ctx_tiles),
        grid=(t // tm,),
        in_specs=[row(D_MODEL),
                  pl.BlockSpec((1, N_MOD, D_MODEL), lambda i: (grp(i), 0, 0)),
                  pl.BlockSpec((6, D_MODEL), lambda i: (0, 0)),
                  row(D_RWKV), row(D_RWKV),
                  pl.BlockSpec(w_out.shape, lambda i: (0, 0))]
                 + [ctx(a) for a in ctx_parts] + [lat(a) for a in lat_parts],
        out_specs=row(D_MODEL),
        out_shape=jax.ShapeDtypeStruct((t, D_MODEL), F32),
        compiler_params=_params("parallel"),
        name="mixer_out",
    )(x, mod, ng, bonus, gate, w_out, *ctx_parts, *lat_parts)


def _rope_perm(d):
    q = d // 4
    base = np.concatenate([np.arange(q, 2 * q), np.arange(0, q)])
    return np.concatenate([base, base + 2 * q])


def _rope_tables(row, col, d):
    half = d // 4
    inv = ROPE_BASE ** (-jnp.arange(half, dtype=F32) / half)
    parts_c, parts_s = [], []
    for pos in (row, col):
        ang = pos.astype(F32)[:, None] * inv
        cos, sin = jnp.cos(ang), jnp.sin(ang)
        parts_c += [cos, cos]
        parts_s += [-sin, sin]
    return jnp.concatenate(parts_c, axis=-1), jnp.concatenate(parts_s, axis=-1)


def _layout_w_in(w_in):
    depth = w_in.shape[0]
    zeros = lambda w: jnp.zeros((depth, D_MODEL, w), w_in.dtype)
    o = 0
    qa = w_in[..., o:o + Q_LORA]; o += Q_LORA
    kva = w_in[..., o:o + KV_LORA]; o += KV_LORA
    kr = w_in[..., o:o + ROPE_DIM]; o += ROPE_DIM
    rw = w_in[..., o:o + 3 * D_RWKV + 4 * LORA_WA + LORA_G]; o += 3 * D_RWKV + 4 * LORA_WA + LORA_G
    rt = w_in[..., o:]
    kr_blk = lambda x: jnp.concatenate([zeros(ROPE_OFF), x, zeros(HEAD_PAD - ROPE_OFF - ROPE_DIM)], axis=-1)
    rw_parts = [rw[..., :3 * D_RWKV]]
    for j in range(4):
        lo = 3 * D_RWKV + j * LORA_WA
        rw_parts += [rw[..., lo:lo + LORA_WA], zeros(LANE - LORA_WA)]
    rw_parts.append(rw[..., 3 * D_RWKV + 4 * LORA_WA:])
    perm = np.concatenate([h * RET_DK + _rope_perm(RET_DK) for h in range(2 * H_RET)])
    cols = [qa, kva, kr_blk(kr), kr_blk(kr[..., _rope_perm(ROPE_DIM)])] + rw_parts + [rt, rt[..., perm]]
    return jnp.concatenate(cols, axis=-1).astype(BF16)


def _layout_mu(mu):
    depth = mu.shape[0]
    parts = [mu[:, :3 * D_RWKV]]
    for j in range(4):
        lo = 3 * D_RWKV + j * LORA_WA
        parts += [mu[:, lo:lo + LORA_WA], jnp.zeros((depth, LANE - LORA_WA), mu.dtype)]
    parts.append(mu[:, 3 * D_RWKV + 4 * LORA_WA:])
    return jnp.concatenate(parts, axis=-1).reshape(depth, 1, RW_W)


def _layout_wq(wq):
    depth = wq.shape[0]
    w = wq.reshape(depth, Q_LORA, H_MLA, QK_NOPE + ROPE_DIM)
    pad = jnp.zeros((depth, Q_LORA, H_MLA, HEAD_PAD - QK_NOPE - ROPE_DIM), wq.dtype)
    nope, rope = w[..., :QK_NOPE], w[..., QK_NOPE:]
    main = jnp.concatenate([nope, rope, pad], axis=-1)
    swap = jnp.concatenate([jnp.zeros_like(nope), rope[..., _rope_perm(ROPE_DIM)], pad], axis=-1)
    shape = (depth, Q_LORA, H_MLA * HEAD_PAD)
    return main.reshape(shape).astype(BF16), swap.reshape(shape).astype(BF16)


def _layout_wkv(wkv):
    depth = wkv.shape[0]
    w = wkv.reshape(depth, KV_LORA, H_MLA, QK_NOPE + V_HD)
    kn = jnp.concatenate([w[..., :QK_NOPE], jnp.zeros((depth, KV_LORA, H_MLA, HEAD_PAD - QK_NOPE), wkv.dtype)], axis=-1)
    return (kn.reshape(depth, KV_LORA, H_MLA * HEAD_PAD).astype(BF16),
            w[..., QK_NOPE:].reshape(depth, KV_LORA, D_MLA).astype(BF16))


def _layout_lora_up(up):
    depth = up.shape[0]
    return jnp.concatenate([up, jnp.zeros((depth, 4, LANE - LORA_WA, D_RWKV), up.dtype)], axis=2).astype(BF16)


def _block_diag(s):
    eye = jnp.eye(H_RWKV, dtype=s.dtype)
    out = jnp.einsum('...hij,hg->...higj', s, eye)
    return out.reshape(s.shape[:-3] + (STACK, STACK))


def _unpack_state(s):
    nb = s.shape[0]
    return s.reshape(nb, 2, RWKV_HD, H_RWKV, RWKV_HD).transpose(0, 1, 3, 2, 4)


def kernel(x_prompt, x_sample, cache_mla_ckv, cache_mla_krope, state_rwkv, state_ret, c, c_ctx,
           norm_g, w_mod, b_mod, ffn_wi, ffn_wo, w_in, w_out, mla_norm_q, mla_norm_kv, mla_wq_up,
           mla_wkv_up, rwkv_mu, rwkv_vec, rwkv_lora_up, rwkv_g_up, ret_gn):
    bc, nc, _ = x_prompt.shape
    bl, nl, _ = x_sample.shape
    depth = norm_g.shape[0]
    past = cache_mla_ckv.shape[2]
    tc, tl = bc * nc, bl * nl
    assert bl <= 7 and nl % GRID_W == 0
    assert nc % CHUNK == 0 and nl % CHUNK == 0 and tc % nl == 0 and tc % past == 0
    assert bc % PAIR == 0 and bl % PAIR == 0

    def tile(cap):
        tm = cap
        while tc % tm or nl % tm:
            tm //= 2
        return tm

    def grouper(tm):
        ctx_tiles, per_seq = tc // tm, nl // tm
        return lambda i: jnp.where(i < ctx_tiles, 0, 1 + jnp.maximum(i - ctx_tiles, 0) // per_seq)

    tm_ffn, tm_tok = tile(1024), tile(512)
    tm_shift = 256
    while nc % tm_shift or nl % tm_shift:
        tm_shift //= 2
    tq_ctx, tq_lat = min(nc, 256), min(nl, 256)

    wi_b, wo_b = ffn_wi.astype(BF16), ffn_wo.astype(BF16)
    w_in_b = _layout_w_in(w_in)
    w_out_b = w_out.astype(BF16)
    wq_b, wqs_b = _layout_wq(mla_wq_up)
    wk_b, wv_b = _layout_wkv(mla_wkv_up)
    mu_l = _layout_mu(rwkv_mu)
    up_b = _layout_lora_up(rwkv_lora_up)
    gup_b = rwkv_g_up.astype(BF16)

    pos = jnp.arange(nl)
    row_l, col_l = pos // GRID_W, pos % GRID_W
    cos_m, sin_m = _rope_tables(row_l, col_l, ROPE_DIM)
    cos_r, sin_r = _rope_tables(row_l, col_l, RET_DK)

    def mla_table(tab, fill):
        lat = jnp.concatenate([jnp.full((nl, ROPE_OFF), fill, F32), tab,
                               jnp.zeros((nl, HEAD_PAD - ROPE_OFF - ROPE_DIM), F32)], axis=-1)
        ctx = jnp.concatenate([jnp.full((tc, ROPE_OFF + ROPE_DIM), fill, F32),
                               jnp.zeros((tc, HEAD_PAD - ROPE_OFF - ROPE_DIM), F32)], axis=-1)
        return jnp.concatenate([ctx, jnp.tile(lat, (bl, 1))], axis=0)

    def ret_table(tab, fill):
        return jnp.concatenate([jnp.full((tc, D_RET), fill, F32), jnp.tile(jnp.tile(tab, (1, H_RET)), (bl, 1))], axis=0)

    cq, sq = mla_table(cos_m, 1.0), mla_table(sin_m, 0.0)
    cr, sr = ret_table(cos_r, 1.0), ret_table(sin_r, 0.0)

    e = 5.0 + jnp.arange(H_RET, dtype=F32)[None, :] + 0.5 * jnp.arange(2, dtype=F32)[:, None]
    lg = jnp.log1p(-jnp.exp2(-e))
    lg_lane = jnp.repeat(lg, RET_DK, axis=1)
    lgm = jnp.broadcast_to(lg_lane[:, :, None], (2, STACK, STACK))
    lgl = lg_lane[:, None, :]

    c_all = jnp.concatenate([c_ctx[None, :], c, jnp.zeros((7 - bl, D_MODEL), F32)], axis=0)
    mod = _modulation(c_all, w_mod, b_mod).reshape(depth, 8, N_MOD, D_MODEL)

    kr_cache = jnp.pad(cache_mla_krope, ((0, 0), (0, 0), (0, 0), (ROPE_OFF, HEAD_PAD - ROPE_OFF - ROPE_DIM)))
    zero_state = jnp.zeros((bc, 2, STACK, STACK), F32)
    s0_rwkv_lat = _block_diag(state_rwkv.astype(F32))
    s0_ret_lat = _block_diag(state_ret.astype(F32))

    x = jnp.concatenate([x_prompt.reshape(tc, D_MODEL), x_sample.reshape(tl, D_MODEL)], axis=0)
    ckv_l, krope_l, rwkv_l, ret_l = [], [], [], []
    for l in range(depth):
        mod_l, ng = mod[l], norm_g[l]
        x = _ffn(x, mod_l, ng, wi_b, wo_b, l, 0, grouper(tm_ffn), tm_ffn)
        qa, kva, kr, krs, rw, rt, rts = _mixin(x, mod_l, ng, w_in_b[l], grouper(tm_tok), tm_tok)

        qh, ckv, kh, vv = _mla_prep(qa, kva, kr, krs, cq, sq, mla_norm_q[l][None, :], mla_norm_kv[l][None, :],
                                    wq_b[l], wqs_b[l], wk_b[l], wv_b[l], tm_tok)
        kh_c, v_c = _cache_kv(cache_mla_ckv[:, l].reshape(bl * past, KV_LORA),
                              kr_cache[:, l].reshape(bl * past, HEAD_PAD), wk_b[l], wv_b[l])
        mla_ctx = _attention(qh, kh, vv, None, None, nb=bc, n=nc, m1=nc, tok0=0, k1_tok0=0, tq=tq_ctx)
        mla_lat = _attention(qh, kh_c, v_c, kh, vv, nb=bl, n=nl, m1=past, tok0=tc, k1_tok0=0, tq=tq_lat)

        r, kk, v, lw, kka, kd, bonus, gate = _rwkv_prep(
            rw, mu_l[l], rwkv_vec[l], up_b[l], gup_b[l],
            tm=tm_shift, ctx_tiles=tc // tm_shift, ctx_tiles_per_seq=nc // tm_shift,
            lat_tiles_per_seq=nl // tm_shift)
        gn_r = rwkv_vec[l][7:8, :]
        yf_c, yb_c, s_rwkv = _rwkv_scan(r, kk, v, lw, kka, kd, zero_state, gn_r, nb=bc, n=nc, tok0=0)
        yf_l, yb_l, _ = _rwkv_scan(r, kk, v, lw, kka, kd, s0_rwkv_lat[:, l], gn_r, nb=bl, n=nl, tok0=tc)

        gn_t = ret_gn[l][:, None, :]
        of_c, ob_c, s_ret = _retention(rt, rts, cr, sr, lgm, lgl, gn_t, zero_state, nb=bc, n=nc, tok0=0)
        of_l, ob_l, _ = _retention(rt, rts, cr, sr, lgm, lgl, gn_t, s0_ret_lat[:, l], nb=bl, n=nl, tok0=tc)

        x = _mixout(x, mod_l, ng, bonus, gate, w_out_b[l], (mla_ctx, yf_c, yb_c, of_c, ob_c),
                    (mla_lat, yf_l, yb_l, of_l, ob_l), grouper(tm_tok), tm_tok, tc // tm_tok)
        x = _ffn(x, mod_l, ng, wi_b, wo_b, l, 1, grouper(tm_ffn), tm_ffn)

        ckv_l.append(ckv[:tc].reshape(bc, nc, KV_LORA))
        krope_l.append(kr[:tc, ROPE_OFF:ROPE_OFF + ROPE_DIM].reshape(bc, nc, ROPE_DIM))
        rwkv_l.append(_unpack_state(s_rwkv))
        ret_l.append(_unpack_state(s_ret))

    return (x[:tc].reshape(bc, nc, D_MODEL), x[tc:].reshape(bl, nl, D_MODEL),
            jnp.stack(ckv_l, axis=1), jnp.stack(krope_l, axis=1),
            jnp.stack(rwkv_l, axis=1), jnp.stack(ret_l, axis=1))
```

```python
import functools

import numpy as np
import jax
import jax.numpy as jnp
from jax import lax
from jax.experimental import pallas as pl
from jax.experimental.pallas import tpu as pltpu

F32 = jnp.float32
BF16 = jnp.bfloat16

D_MODEL = 1024
N_MOD = 9
D_FF = 2816
H_MLA = 8
QK_NOPE = 64
ROPE_DIM = 32
V_HD = 64
Q_LORA = 256
KV_LORA = 128
D_MLA = H_MLA * V_HD
MLA_SCALE = (QK_NOPE + ROPE_DIM) ** -0.5
LOG2E = 1.4426950408889634
H_RWKV = 4
RWKV_HD = 64
D_RWKV = H_RWKV * RWKV_HD
LORA_WA = 64
LORA_G = 128
DECAY_SCALE = 0.6065306597126334
RWKV_GN_EPS = 64e-5
H_RET = 4
RET_DK = 64
D_RET = H_RET * 64
RET_GN_EPS = 1e-5
ROPE_BASE = 10000.0
NORM_EPS = 1e-6
GRID_W = 64

LANE = 128
HEAD_PAD = 128
ROPE_OFF = QK_NOPE
CHUNK = 64
STACK = H_RWKV * CHUNK
PAIR = 4
ATTN_HEADS = 4
ATTN_KEY_BLOCK = 1024
RWKV_GROUP = 2
GSTACK = RWKV_GROUP * CHUNK
VMEM_LIMIT = 56 * 1024 * 1024

C_QA = 0
C_KVA = C_QA + Q_LORA
C_KR = C_KVA + KV_LORA
C_KRS = C_KR + LANE
C_RW = C_KRS + LANE
RW_W = 3 * D_RWKV + 4 * LANE + LORA_G
C_RT = C_RW + RW_W
RT_W = 2 * H_RET * RET_DK + 3 * D_RET
C_RTS = C_RT + RT_W
RTS_W = 2 * H_RET * RET_DK
IN_W = C_RTS + RTS_W


def _params(*sem):
    return pltpu.CompilerParams(dimension_semantics=sem, vmem_limit_bytes=VMEM_LIMIT)


def _sigmoid(x):
    return 1.0 / (1.0 + jnp.exp(-x))


def _rms(x, g):
    return x * lax.rsqrt(jnp.mean(x * x, axis=-1, keepdims=True) + NORM_EPS) * g


def _bdot(a, b):
    return jnp.dot(a.astype(BF16), b.astype(BF16), preferred_element_type=F32)


def _bdot_nt(a, b):
    return lax.dot_general(a.astype(BF16), b.astype(BF16), (((1,), (1,)), ((), ())),
                           preferred_element_type=F32)


def _bdot_tn(a, b):
    return lax.dot_general(a.astype(BF16), b.astype(BF16), (((0,), (0,)), ((), ())),
                           preferred_element_type=F32)


def _split_bf16(x):
    hi = x.astype(BF16)
    return hi, (x - hi.astype(F32)).astype(BF16)


def _dot_exact_rhs(x, m):
    hi, lo = _split_bf16(x)
    mb = m.astype(BF16)
    return jnp.dot(hi, mb, preferred_element_type=F32) + jnp.dot(lo, mb, preferred_element_type=F32)


def _dot_exact_lhs(m, x):
    hi, lo = _split_bf16(x)
    mb = m.astype(BF16)
    return jnp.dot(mb, hi, preferred_element_type=F32) + jnp.dot(mb, lo, preferred_element_type=F32)


def _head_mean_matrix():
    r = lax.broadcasted_iota(jnp.int32, (STACK, STACK), 0) // RWKV_HD
    c = lax.broadcasted_iota(jnp.int32, (STACK, STACK), 1) // RWKV_HD
    return jnp.where(r == c, 1.0 / RWKV_HD, 0.0).astype(F32)


def _head_groupnorm(y, gain, eps):
    mm = _head_mean_matrix()
    yc = y - _dot_exact_rhs(y, mm)
    var = _dot_exact_rhs(yc * yc, mm)
    return yc * lax.rsqrt(var + eps) * gain


def _stack_heads(x):
    head = lax.broadcasted_iota(jnp.int32, x.shape, 1) // RWKV_HD
    return jnp.concatenate([jnp.where(head == h, x, 0.0) for h in range(H_RWKV)], axis=0)


def _unstack_heads(x):
    out = x[0:CHUNK]
    for h in range(1, H_RWKV):
        out = out + x[h * CHUNK:(h + 1) * CHUNK]
    return out


def _stack_group(x, grp):
    head = lax.broadcasted_iota(jnp.int32, x.shape, 1) // RWKV_HD
    return jnp.concatenate([jnp.where(head == grp * RWKV_GROUP + j, x, 0.0) for j in range(RWKV_GROUP)], axis=0)


def _unstack_group(x):
    out = x[0:CHUNK]
    for j in range(1, RWKV_GROUP):
        out = out + x[j * CHUNK:(j + 1) * CHUNK]
    return out


def _mod_body(c_ref, w_ref, b_ref, o_ref):
    c = c_ref[...]
    s = c * _sigmoid(c)
    o_ref[0] = _bdot(s, w_ref[0]) + b_ref[0]


def _modulation(c_all, w_mod, b_mod):
    depth, d, nd = w_mod.shape
    tn = D_MODEL
    return pl.pallas_call(
        _mod_body,
        grid=(depth, nd // tn),
        in_specs=[pl.BlockSpec((8, d), lambda l, j: (0, 0)),
                  pl.BlockSpec((1, d, tn), lambda l, j: (l, 0, j)),
                  pl.BlockSpec((1, 1, tn), lambda l, j: (l, 0, j))],
        out_specs=pl.BlockSpec((1, 8, tn), lambda l, j: (l, 0, j)),
        out_shape=jax.ShapeDtypeStruct((depth, 8, nd), F32),
        compiler_params=_params("parallel", "parallel"),
        name="modulation",
    )(c_all, w_mod, b_mod.reshape(depth, 1, nd))


def _ffn_body(x_ref, mod_ref, ng_ref, wg_ref, wu_ref, wo_ref, o_ref, h_ref, acc_ref, *, mrow, grow):
    f = pl.program_id(1)

    @pl.when(f == 0)
    def _():
        m = mod_ref[0]
        h = _rms(x_ref[...], ng_ref[grow:grow + 1, :]) * (1.0 + m[mrow + 1:mrow + 2, :]) + m[mrow:mrow + 1, :]
        h_ref[...] = h.astype(BF16)
        acc_ref[...] = jnp.zeros_like(acc_ref)

    h = h_ref[...]
    g = jnp.dot(h, wg_ref[...], preferred_element_type=F32)
    u = jnp.dot(h, wu_ref[...], preferred_element_type=F32)
    a = (g * _sigmoid(g)) * u
    acc_ref[...] += jnp.dot(a.astype(BF16), wo_ref[...], preferred_element_type=F32)

    @pl.when(f == pl.num_programs(1) - 1)
    def _():
        m = mod_ref[0]
        o_ref[...] = x_ref[...] + 0.5 * m[mrow + 2:mrow + 3, :] * _rms(acc_ref[...], ng_ref[grow + 1:grow + 2, :])


def _ffn(x, mod, ng, wi, wo, layer, which, grp, tm):
    t = x.shape[0]
    tf = 256
    nf = D_FF // tf
    body = functools.partial(_ffn_body, mrow=6 * which, grow=4 * which)
    return pl.pallas_call(
        body,
        grid=(t // tm, nf),
        in_specs=[pl.BlockSpec((tm, D_MODEL), lambda i, f: (i, 0)),
                  pl.BlockSpec((1, N_MOD, D_MODEL), lambda i, f: (grp(i), 0, 0)),
                  pl.BlockSpec((6, D_MODEL), lambda i, f: (0, 0)),
                  pl.BlockSpec((None, None, D_MODEL, tf), lambda i, f: (layer, which, 0, f)),
                  pl.BlockSpec((None, None, D_MODEL, tf), lambda i, f: (layer, which, 0, f + nf)),
                  pl.BlockSpec((None, None, tf, D_MODEL), lambda i, f: (layer, which, f, 0))],
        out_specs=pl.BlockSpec((tm, D_MODEL), lambda i, f: (i, 0)),
        out_shape=jax.ShapeDtypeStruct((t, D_MODEL), F32),
        scratch_shapes=[pltpu.VMEM((tm, D_MODEL), BF16), pltpu.VMEM((tm, D_MODEL), F32)],
        compiler_params=_params("parallel", "arbitrary"),
        name="ffn",
    )(x, mod, ng, wi, wi, wo)


def _mixin_body(x_ref, mod_ref, ng_ref, w_ref, cq_ref, sq_ref, gq_ref, gkv_ref, wq_ref, wqs_ref, wk_ref, wv_ref,
                kr_ref, rw_ref, rt_ref, rts_ref, qh_ref, ckv_ref, kh_ref, v_ref):
    m = mod_ref[0]
    h = (_rms(x_ref[...], ng_ref[2:3, :]) * (1.0 + m[4:5, :]) + m[3:4, :]).astype(BF16)

    def proj(lo, width):
        return jnp.dot(h, w_ref[:, lo:lo + width], preferred_element_type=F32)

    rw_ref[...] = proj(C_RW, RW_W)
    rt_ref[...] = proj(C_RT, RT_W)
    rts_ref[...] = proj(C_RTS, RTS_W)

    kr = proj(C_KR, LANE)
    kr_ref[...] = kr
    cq = cq_ref[...]
    sq = sq_ref[...]
    qn = _rms(proj(C_QA, Q_LORA), gq_ref[...]).astype(BF16)
    q = jnp.dot(qn, wq_ref[...], preferred_element_type=F32)
    qs = jnp.dot(qn, wqs_ref[...], preferred_element_type=F32)
    for hd in range(H_MLA):
        sl = slice(hd * HEAD_PAD, (hd + 1) * HEAD_PAD)
        qh_ref[hd] = (q[:, sl] * cq + qs[:, sl] * sq).astype(BF16)
    ckv = _rms(proj(C_KVA, KV_LORA), gkv_ref[...])
    ckv_ref[...] = ckv
    krr = kr * cq + proj(C_KRS, LANE) * sq
    _kv_expand(ckv, krr, wk_ref, wv_ref, kh_ref, v_ref)


def _mixin(x, mod, ng, w_in, cq, sq, gq, gkv, wq, wqs, wk, wv, grp, tm):
    t = x.shape[0]
    row = lambda w: pl.BlockSpec((tm, w), lambda i: (i, 0))
    full = lambda a: pl.BlockSpec(a.shape, lambda i: (0,) * a.ndim)
    heads = lambda n: pl.BlockSpec((n, tm, LANE), lambda i: (0, i, 0))
    f32 = lambda w: jax.ShapeDtypeStruct((t, w), F32)
    return pl.pallas_call(
        _mixin_body,
        grid=(t // tm,),
        in_specs=[row(D_MODEL),
                  pl.BlockSpec((1, N_MOD, D_MODEL), lambda i: (grp(i), 0, 0)),
                  pl.BlockSpec((6, D_MODEL), lambda i: (0, 0)),
                  full(w_in), row(LANE), row(LANE), full(gq), full(gkv), full(wq), full(wqs), full(wk), full(wv)],
        out_specs=[row(LANE), row(RW_W), row(RT_W), row(RTS_W),
                   heads(H_MLA), row(KV_LORA), heads(H_MLA), heads(H_MLA // 2)],
        out_shape=[f32(LANE), f32(RW_W), f32(RT_W), f32(RTS_W),
                   jax.ShapeDtypeStruct((H_MLA, t, HEAD_PAD), BF16), f32(KV_LORA),
                   jax.ShapeDtypeStruct((H_MLA, t, HEAD_PAD), BF16),
                   jax.ShapeDtypeStruct((H_MLA // 2, t, LANE), BF16)],
        compiler_params=_params("parallel"),
        name="mixer_in",
    )(x, mod, ng, w_in, cq, sq, gq, gkv, wq, wqs, wk, wv)


def _kv_expand(ckv, krr, wk_ref, wv_ref, kh_ref, v_ref):
    c = ckv.astype(BF16)
    kn = jnp.dot(c, wk_ref[...], preferred_element_type=F32)
    v = jnp.dot(c, wv_ref[...], preferred_element_type=F32)
    for h in range(H_MLA):
        kh_ref[h] = (kn[:, h * HEAD_PAD:(h + 1) * HEAD_PAD] + krr).astype(BF16)
    for p in range(H_MLA // 2):
        v_ref[p] = v[:, p * LANE:(p + 1) * LANE].astype(BF16)


def _cache_kv_body(ckv_ref, kr_ref, wk_ref, wv_ref, kh_ref, v_ref):
    _kv_expand(ckv_ref[...], kr_ref[...], wk_ref, wv_ref, kh_ref, v_ref)


def _cache_kv(ckv, kr, wk, wv):
    t = ckv.shape[0]
    tm = min(t, 512)
    row = lambda w: pl.BlockSpec((tm, w), lambda i: (i, 0))
    full = lambda a: pl.BlockSpec(a.shape, lambda i: (0,) * a.ndim)
    return pl.pallas_call(
        _cache_kv_body,
        grid=(t // tm,),
        in_specs=[row(KV_LORA), row(LANE), full(wk), full(wv)],
        out_specs=[pl.BlockSpec((H_MLA, tm, HEAD_PAD), lambda i: (0, i, 0)),
                   pl.BlockSpec((H_MLA // 2, tm, LANE), lambda i: (0, i, 0))],
        out_shape=[jax.ShapeDtypeStruct((H_MLA, t, HEAD_PAD), BF16),
                   jax.ShapeDtypeStruct((H_MLA // 2, t, LANE), BF16)],
        compiler_params=_params("parallel"),
        name="cache_kv",
    )(ckv, kr, wk, wv)


def _attn_body(*refs, two):
    if two:
        q_ref, k1_ref, v1_ref, k2_ref, v2_ref, o_ref = refs
    else:
        q_ref, k1_ref, v1_ref, o_ref = refs
    nt = (((1,), (1,)), ((), ()))
    blocks = [(k1_ref, v1_ref, 0, k1_ref.shape[1])]
    if two:
        kb = min(ATTN_KEY_BLOCK, k2_ref.shape[1])
        blocks += [(k2_ref, v2_ref, lo, lo + kb) for lo in range(0, k2_ref.shape[1], kb)]

    s_blk, m_row, p_blk, l_row, acc, outs = {}, {}, {}, {}, {}, []
    for t in range(ATTN_HEADS + 2):
        for bi, (k_ref, v_ref, lo, hi) in enumerate(blocks):
            if t < ATTN_HEADS:
                s = lax.dot_general(q_ref[t], k_ref[t, lo:hi, :], nt, preferred_element_type=F32)
                s_blk[t, bi] = s
                mb = jnp.max(s, axis=-1, keepdims=True)
                m_row[t] = mb if bi == 0 else jnp.maximum(m_row[t], mb)
            if 1 <= t <= ATTN_HEADS:
                p = jnp.exp2((s_blk.pop((t - 1, bi)) - m_row[t - 1]) * (MLA_SCALE * LOG2E))
                lb = jnp.sum(p, axis=-1, keepdims=True)
                l_row[t - 1] = lb if bi == 0 else l_row[t - 1] + lb
                p_blk[t - 1, bi] = p.astype(BF16)
            if t >= 2:
                ob = jnp.dot(p_blk.pop((t - 2, bi)), v_ref[(t - 2) // 2, lo:hi, :], preferred_element_type=F32)
                acc[t - 2] = ob if bi == 0 else acc[t - 2] + ob
        if t >= 2:
            outs.append(acc.pop(t - 2) / l_row.pop(t - 2))
    lane = lax.broadcasted_iota(jnp.int32, outs[0].shape, 1)
    for p in range(ATTN_HEADS // 2):
        o_ref[:, p * LANE:(p + 1) * LANE] = jnp.where(lane < V_HD, outs[2 * p], outs[2 * p + 1])


def _attention(qh, kh1, v1, kh2, v2, *, nb, n, m1, tok0, k1_tok0, tq):
    two = kh2 is not None
    nq = n // tq
    qb0 = tok0 // tq
    hp, vp = ATTN_HEADS, ATTN_HEADS // 2
    in_specs = [pl.BlockSpec((hp, tq, HEAD_PAD), lambda b, p, i: (p, qb0 + b * nq + i, 0)),
                pl.BlockSpec((hp, m1, HEAD_PAD), lambda b, p, i: (p, k1_tok0 // m1 + b, 0)),
                pl.BlockSpec((vp, m1, LANE), lambda b, p, i: (p, k1_tok0 // m1 + b, 0))]
    args = [qh, kh1, v1]
    if two:
        in_specs += [pl.BlockSpec((hp, n, HEAD_PAD), lambda b, p, i: (p, tok0 // n + b, 0)),
                     pl.BlockSpec((vp, n, LANE), lambda b, p, i: (p, tok0 // n + b, 0))]
        args += [kh2, v2]
    return pl.pallas_call(
        functools.partial(_attn_body, two=two),
        grid=(nb, H_MLA // hp, nq),
        in_specs=in_specs,
        out_specs=pl.BlockSpec((tq, vp * LANE), lambda b, p, i: (b * nq + i, p)),
        out_shape=jax.ShapeDtypeStruct((nb * n, D_MLA), F32),
        compiler_params=_params("parallel", "parallel", "parallel"),
        name="mla_attention",
    )(*args)


def _rwkv_prep_body(x_ref, xp_ref, xn_ref, mu_ref, vec_ref, up_ref, gup_ref,
                    r_ref, kk_ref, v_ref, lw_ref, kka_ref, kd_ref, bonus_ref, gate_ref,
                    *, tm, ctx_tiles, ctx_tiles_per_seq, lat_tiles_per_seq):
    i = pl.program_id(0)
    in_ctx = i < ctx_tiles
    per_seq = jnp.where(in_ctx, ctx_tiles_per_seq, lat_tiles_per_seq)
    j = jnp.where(in_ctx, i % ctx_tiles_per_seq, jnp.maximum(i - ctx_tiles, 0) % lat_tiles_per_seq)
    first = j == 0
    last = j == per_seq - 1
    x = x_ref[...]
    rowi = lax.broadcasted_iota(jnp.int32, x.shape, 0)
    prev_row = jnp.where(first, 0.0, xp_ref[7:8, :])
    next_row = jnp.where(last, 0.0, xn_ref[0:1, :])
    xprev = jnp.where(rowi == 0, prev_row, pltpu.roll(x, 1, 0))
    xnext = jnp.where(rowi == tm - 1, next_row, pltpu.roll(x, tm - 1, 0))
    xr = x + mu_ref[...] * (0.5 * (xprev + xnext) - x)

    vec = vec_ref[...]
    r = xr[:, 0:D_RWKV]
    kr = xr[:, D_RWKV:2 * D_RWKV]
    vr = xr[:, 2 * D_RWKV:3 * D_RWKV]
    o = 3 * D_RWKV
    zw = (xr[:, o:o + LANE], xr[:, o + LANE:o + 2 * LANE])
    za = (xr[:, o + 2 * LANE:o + 3 * LANE], xr[:, o + 3 * LANE:o + 4 * LANE])
    zg = xr[:, o + 4 * LANE:o + 5 * LANE]

    mm = _head_mean_matrix() * float(RWKV_HD)
    kk = kr * vec[4:5, :]
    kk = kk / (jnp.sqrt(_dot_exact_rhs(kk * kk, mm)) + 1e-12)
    r_ref[...] = r
    kk_ref[...] = kk
    v_ref[...] = vr
    gate_ref[...] = _bdot(_sigmoid(zg), gup_ref[...])
    bonus = jnp.zeros_like(r)
    for z in range(2):
        zwz = _bdot(jnp.tanh(zw[z]), up_ref[z]) + vec[z:z + 1, :]
        lw_ref[z] = -DECAY_SCALE * _sigmoid(zwz)
        a = _sigmoid(_bdot(za[z], up_ref[2 + z]) + vec[2 + z:3 + z, :])
        kd = kr * (1.0 + (a - 1.0) * vec[5:6, :])
        kka_ref[z] = kk * a
        kd_ref[z] = kd
        bonus = bonus + _dot_exact_rhs(r * kd * vec[6:7, :], mm) * vr
    bonus_ref[...] = bonus


def _rwkv_prep(rw, mu, vec, up, gup, *, tm, ctx_tiles, ctx_tiles_per_seq, lat_tiles_per_seq):
    t = rw.shape[0]
    hb = tm // 8
    nhb = t // 8
    row = lambda: pl.BlockSpec((tm, D_RWKV), lambda i: (i, 0))
    two = lambda: pl.BlockSpec((2, tm, D_RWKV), lambda i: (0, i, 0))
    full = lambda a: pl.BlockSpec(a.shape, lambda i: (0,) * a.ndim)
    body = functools.partial(_rwkv_prep_body, tm=tm, ctx_tiles=ctx_tiles, ctx_tiles_per_seq=ctx_tiles_per_seq,
                             lat_tiles_per_seq=lat_tiles_per_seq)
    one = jax.ShapeDtypeStruct((t, D_RWKV), F32)
    both = jax.ShapeDtypeStruct((2, t, D_RWKV), F32)
    return pl.pallas_call(
        body,
        grid=(t // tm,),
        in_specs=[pl.BlockSpec((tm, RW_W), lambda i: (i, 0)),
                  pl.BlockSpec((8, RW_W), lambda i: (jnp.maximum(i * hb - 1, 0), 0)),
                  pl.BlockSpec((8, RW_W), lambda i: (jnp.minimum((i + 1) * hb, nhb - 1), 0)),
                  full(mu), full(vec), full(up), full(gup)],
        out_specs=[row(), row(), row(), two(), two(), two(), row(), row()],
        out_shape=[one, one, one, both, both, both, one, one],
        compiler_params=_params("parallel"),
        name="rwkv_prep",
    )(rw, rw, rw, mu, vec, up, gup)


def _unit_lower_inverse(lmats, rowi, coli):
    eye = jnp.where(rowi == coli, 1.0, 0.0).astype(F32)
    n1 = [jnp.where((rowi >> 3) == (coli >> 3), lm, 0.0) for lm in lmats]
    n2 = [_bdot(x, x) for x in n1]
    n4 = [_bdot(x, x) for x in n2]
    t = [_bdot(eye - x, eye + y) for x, y in zip(n1, n2)]
    t = [_bdot(x, eye + y) for x, y in zip(t, n4)]
    for sh in (4, 5, 6):
        inner = (rowi >> (sh - 1)) == (coli >> (sh - 1))
        outer = (rowi >> sh) == (coli >> sh)
        off = jnp.logical_and(outer, jnp.logical_not(inner))
        tb = [x.astype(BF16) for x in t]
        et = [_bdot(jnp.where(off, lm, 0.0), x) for lm, x in zip(lmats, tb)]
        t = [x - _bdot(xb, y) for x, xb, y in zip(t, tb, et)]
    return t


def _scan_masks(z):
    sgn = 1 - 2 * z
    rowi = lax.broadcasted_iota(jnp.int32, (GSTACK, GSTACK), 0)
    coli = lax.broadcasted_iota(jnp.int32, (GSTACK, GSTACK), 1)
    strict = ((coli & (CHUNK - 1)) - (rowi & (CHUNK - 1))) * sgn < 0
    row2 = lax.broadcasted_iota(jnp.int32, (GSTACK, 2 * GSTACK), 0)
    col2 = lax.broadcasted_iota(jnp.int32, (GSTACK, 2 * GSTACK), 1)
    incl2 = ((col2 & (CHUNK - 1)) - (row2 & (CHUNK - 1))) * sgn <= 0
    ri = lax.broadcasted_iota(jnp.int32, (CHUNK, CHUNK), 0)
    ci = lax.broadcasted_iota(jnp.int32, (CHUNK, CHUNK), 1)
    cmat = jnp.where((ci - ri) * sgn <= 0, 1.0, 0.0).astype(F32)
    return strict, incl2, cmat


def _rwkv_steps(chains):
    gs = GSTACK
    rowi = lax.broadcasted_iota(jnp.int32, (gs, gs), 0)
    coli = lax.broadcasted_iota(jnp.int32, (gs, gs), 1)
    groups = range(H_RWKV // RWKV_GROUP)
    aq, bc, vs, wtot, owner = [], [], [], [], []
    for ci, (r, kk, v, lw, kka, kd, s, (strict, incl2, cmat)) in enumerate(chains):
        cum = _dot_exact_lhs(cmat, lw)
        iw = jnp.exp(-cum)
        a = jnp.exp(cum - lw) * kk
        q = r * jnp.exp(cum)
        b, cm = kka * iw, kd * iw
        wtot.append(jnp.exp(jnp.sum(lw, axis=0, keepdims=True)))
        for grp in groups:
            aq.append(jnp.concatenate([_stack_group(a, grp), _stack_group(q, grp)], axis=0).astype(BF16))
            bc.append(jnp.concatenate([_stack_group(b, grp), _stack_group(cm, grp)], axis=0).astype(BF16))
            vs.append(_stack_group(v, grp).astype(BF16))
            owner.append(chains[ci])
    g = [_bdot_nt(x, y) for x, y in zip(aq, bc)]
    aqs = [_bdot_nt(x, ch[6]) for x, ch in zip(aq, owner)]
    lab = [jnp.where(ch[7][0], x[0:gs, 0:gs], 0.0) for x, ch in zip(g, owner)]
    lac = [jnp.where(ch[7][0], x[0:gs, gs:], 0.0) for x, ch in zip(g, owner)]
    rhs = [x[0:gs] + _bdot(y, v) for x, y, v in zip(aqs, lac, vs)]
    tinv = _unit_lower_inverse(lab, rowi, coli)
    uv = [jnp.concatenate([(-_bdot(t, x)).astype(BF16), v], axis=0) for t, x, v in zip(tinv, rhs, vs)]
    ys = [ax[gs:] + _bdot(jnp.where(ch[7][1], gx[gs:, :], 0.0), uvx) for ch, gx, ax, uvx in zip(owner, g, aqs, uv)]
    ds = [_bdot_tn(uvx, bcx) for uvx, bcx in zip(uv, bc)]
    out, ng = [], len(groups)
    for ci, ch in enumerate(chains):
        y = sum(_unstack_group(t) for t in ys[ci * ng:(ci + 1) * ng])
        s_new = (ch[6] + sum(ds[ci * ng:(ci + 1) * ng])) * wtot[ci]
        out.append((y, s_new))
    return out


def _rwkv_scan_body(*refs):
    n_in = 6 * 2 * PAIR
    s0_ref, gn_ref, yf_ref, yb_ref, sfin_ref, s_ref = refs[n_in:]
    c = pl.program_id(1)

    @pl.when(c == 0)
    def _():
        s_ref[...] = s0_ref[...]

    masks = [_scan_masks(z) for z in range(2)]
    ids = [(bb, z) for bb in range(PAIR) for z in range(2)]
    chains = []
    for bb, z in ids:
        r_ref, kk_ref, v_ref, lw_ref, kka_ref, kd_ref = refs[6 * (2 * bb + z):6 * (2 * bb + z) + 6]
        chains.append((r_ref[...], kk_ref[...], v_ref[...], lw_ref[0], kka_ref[0], kd_ref[0], s_ref[bb, z], masks[z]))
    results = _rwkv_steps(chains)
    gn = gn_ref[...]
    for (bb, z), (y, s_new) in zip(ids, results):
        s_ref[bb, z] = s_new
        (yf_ref, yb_ref)[z][0, bb] = _head_groupnorm(y, gn, RWKV_GN_EPS)

    @pl.when(c == pl.num_programs(1) - 1)
    def _():
        for (bb, z), (_, s_new) in zip(ids, results):
            sfin_ref[bb, z] = _unstack_heads(s_new)


def _scan_specs(nb, n, tok0):
    nc = n // CHUNK
    cb0 = tok0 // CHUNK

    def cidx(bb, z):
        return lambda i, c: cb0 + (PAIR * i + bb) * nc + c + z * (nc - 1 - 2 * c)

    state = pl.BlockSpec((PAIR, 2, STACK, STACK), lambda i, c: (i, 0, 0, 0))
    out_specs = [pl.BlockSpec((1, PAIR, CHUNK, D_RWKV), lambda i, c: (i, 0, c, 0)),
                 pl.BlockSpec((1, PAIR, CHUNK, D_RWKV), lambda i, c: (i, 0, nc - 1 - c, 0)),
                 pl.BlockSpec((PAIR, 2, RWKV_HD, STACK), lambda i, c: (i, 0, 0, 0))]
    out_shape = [jax.ShapeDtypeStruct((nb // PAIR, PAIR, n, D_RWKV), F32),
                 jax.ShapeDtypeStruct((nb // PAIR, PAIR, n, D_RWKV), F32),
                 jax.ShapeDtypeStruct((nb, 2, RWKV_HD, STACK), F32)]
    return nc, cidx, state, out_specs, out_shape


def _rwkv_scan(r, kk, v, lw, kka, kd, s0, gn, *, nb, n, tok0):
    nc, cidx, state, out_specs, out_shape = _scan_specs(nb, n, tok0)
    in_specs, args = [], []
    for bb in range(PAIR):
        for z in range(2):
            ci = cidx(bb, z)
            for arr in (r, kk, v):
                in_specs.append(pl.BlockSpec((CHUNK, D_RWKV), lambda i, c, ci=ci: (ci(i, c), 0)))
                args.append(arr)
            for arr in (lw, kka, kd):
                in_specs.append(pl.BlockSpec((1, CHUNK, D_RWKV), lambda i, c, ci=ci, z=z: (z, ci(i, c), 0)))
                args.append(arr)
    in_specs += [state, pl.BlockSpec((1, D_RWKV), lambda i, c: (0, 0))]
    y_f, y_b, sfin = pl.pallas_call(
        _rwkv_scan_body,
        grid=(nb // PAIR, nc),
        in_specs=in_specs,
        out_specs=out_specs,
        out_shape=out_shape,
        scratch_shapes=[pltpu.VMEM((PAIR, 2, STACK, STACK), F32)],
        compiler_params=_params("parallel", "arbitrary"),
        name="rwkv_scan",
    )(*args, s0, gn)
    return y_f.reshape(nb * n, D_RWKV), y_b.reshape(nb * n, D_RWKV), sfin


def _ret_consts(z, lgm, lgl):
    sgn = 1 - 2 * z
    ti = lax.broadcasted_iota(jnp.int32, (CHUNK, D_RET), 0)
    pos = (ti if z == 0 else CHUNK - 1 - ti).astype(F32)
    xi = jnp.exp(lgl * (pos + 1.0))
    zeta = jnp.exp(lgl * (CHUNK - 1.0 - pos))
    rowi = lax.broadcasted_iota(jnp.int32, (STACK, STACK), 0)
    coli = lax.broadcasted_iota(jnp.int32, (STACK, STACK), 1)
    dist = ((rowi & (CHUNK - 1)) - (coli & (CHUNK - 1))) * sgn
    dmask = jnp.where(dist >= 0, jnp.exp(lgm * jnp.maximum(dist, 0).astype(F32)), 0.0)
    return dmask, xi, zeta, jnp.exp(lgl * float(CHUNK))


def _ret_steps(chains):
    qr = [q * cr + qs * sr for q, k, v, g, qs, ks, cr, sr, s, cst, gn in chains]
    kr = [(k * cr + ks * sr) * (RET_DK ** -0.5) for q, k, v, g, qs, ks, cr, sr, s, cst, gn in chains]
    vs = [_stack_heads(ch[2]).astype(BF16) for ch in chains]
    scores = [_bdot_nt(_stack_heads(q), _stack_heads(k)) * ch[9][0] for q, k, ch in zip(qr, kr, chains)]
    cross = [_bdot(q * ch[9][1], ch[8]) for q, ch in zip(qr, chains)]
    inner = [_bdot(sc, v) for sc, v in zip(scores, vs)]
    kv = [_bdot_tn(_stack_heads(k * ch[9][2]), v) for k, v, ch in zip(kr, vs, chains)]
    out = []
    for ch, inn, crs, kvx in zip(chains, inner, cross, kv):
        g = ch[3]
        o = _head_groupnorm(_unstack_heads(inn) + crs, ch[10], RET_GN_EPS) * (g * _sigmoid(g))
        out.append((o, ch[8] * ch[9][3] + kvx))
    return out


def _ret_body(*refs):
    n_in = 8 * 2 * PAIR
    lgm_ref, lgl_ref, gn_ref, s0_ref, of_ref, ob_ref, sfin_ref, s_ref = refs[n_in:]
    c = pl.program_id(1)

    @pl.when(c == 0)
    def _():
        s_ref[...] = s0_ref[...]

    consts = [_ret_consts(z, lgm_ref[z], lgl_ref[z]) for z in range(2)]
    ids = [(bb, z) for bb in range(PAIR) for z in range(2)]
    chains = []
    for bb, z in ids:
        ins = [ref[...] for ref in refs[8 * (2 * bb + z):8 * (2 * bb + z) + 8]]
        chains.append((*ins, s_ref[bb, z], consts[z], gn_ref[z]))
    results = _ret_steps(chains)
    for (bb, z), (o, s_new) in zip(ids, results):
        s_ref[bb, z] = s_new
        (of_ref, ob_ref)[z][0, bb] = o

    @pl.when(c == pl.num_programs(1) - 1)
    def _():
        for (bb, z), (_, s_new) in zip(ids, results):
            sfin_ref[bb, z] = _unstack_heads(s_new)


def _retention(rt, rts, cr, sr, lgm, lgl, gn, s0, *, nb, n, tok0):
    nc, cidx, state, out_specs, out_shape = _scan_specs(nb, n, tok0)
    in_specs, args = [], []
    for bb in range(PAIR):
        for z in range(2):
            ci = cidx(bb, z)
            for arr, col in ((rt, 0), (rt, 1), (rt, 2), (rt, 3 + z), (rts, 0), (rts, 1), (cr, 0), (sr, 0)):
                in_specs.append(pl.BlockSpec((CHUNK, D_RET), lambda i, c, ci=ci, col=col: (ci(i, c), col)))
                args.append(arr)
    full = lambda a: pl.BlockSpec(a.shape, lambda i, c: (0,) * a.ndim)
    in_specs += [full(lgm), full(lgl), full(gn), state]
    o_f, o_b, sfin = pl.pallas_call(
        _ret_body,
        grid=(nb // PAIR, nc),
        in_specs=in_specs,
        out_specs=out_specs,
        out_shape=out_shape,
        scratch_shapes=[pltpu.VMEM((PAIR, 2, STACK, STACK), F32)],
        compiler_params=_params("parallel", "arbitrary"),
        name="retention",
    )(*args, lgm, lgl, gn, s0)
    return o_f.reshape(nb * n, D_RET), o_b.reshape(nb * n, D_RET), sfin


def _mixout_body(x_ref, mod_ref, ng_ref, bonus_ref, gate_ref, w_ref, *refs, ctx_tiles):
    ctx_refs, lat_refs, o_ref = refs[0:5], refs[5:10], refs[10]
    in_ctx = pl.program_id(0) < ctx_tiles
    mla, yf, yb, of, ob = [jnp.where(in_ctx, a[...], b[...]) for a, b in zip(ctx_refs, lat_refs)]
    m = mod_ref[0]
    rwkv_o = (yf + yb + bonus_ref[...]) * gate_ref[...]
    ret_o = of + ob
    mixed = (_bdot(mla, w_ref[0:D_MLA, :])
             + _bdot(rwkv_o, w_ref[D_MLA:D_MLA + D_RWKV, :])
             + _bdot(ret_o, w_ref[D_MLA + D_RWKV:, :]))
    o_ref[...] = x_ref[...] + m[5:6, :] * _rms(mixed, ng_ref[3:4, :])


def _mixout(x, mod, ng, bonus, gate, w_out, ctx_parts, lat_parts, grp, tm, ctx_tiles):
    t = x.shape[0]
    row = lambda w: pl.BlockSpec((tm, w), lambda i: (i, 0))
    ctx = lambda a: pl.BlockSpec((tm, a.shape[1]), lambda i: (jnp.minimum(i, ctx_tiles - 1), 0))
    lat = lambda a: pl.BlockSpec((tm, a.shape[1]), lambda i: (jnp.maximum(i - ctx_tiles, 0), 0))
    return pl.pallas_call(
        functools.partial(_mixout_body, ctx_tiles=ctx_tiles),
        grid=(t // tm,),
        in_specs=[row(D_MODEL),
                  pl.BlockSpec((1, N_MOD, D_MODEL), lambda i: (grp(i), 0, 0)),
                  pl.BlockSpec((6, D_MODEL), lambda i: (0, 0)),
                  row(D_RWKV), row(D_RWKV),
                  pl.BlockSpec(w_out.shape, lambda i: (0, 0))]
                 + [ctx(a) for a in ctx_parts] + [lat(a) for a in lat_parts],
        out_specs=row(D_MODEL),
        out_shape=jax.ShapeDtypeStruct((t, D_MODEL), F32),
        compiler_params=_params("parallel"),
        name="mixer_out",
    )(x, mod, ng, bonus, gate, w_out, *ctx_parts, *lat_parts)


def _rope_perm(d):
    q = d // 4
    base = np.concatenate([np.arange(q, 2 * q), np.arange(0, q)])
    return np.concatenate([base, base + 2 * q])


def _rope_tables(row, col, d):
    half = d // 4
    inv = ROPE_BASE ** (-jnp.arange(half, dtype=F32) / half)
    parts_c, parts_s = [], []
    for pos in (row, col):
        ang = pos.astype(F32)[:, None] * inv
        cos, sin = jnp.cos(ang), jnp.sin(ang)
        parts_c += [cos, cos]
        parts_s += [-sin, sin]
    return jnp.concatenate(parts_c, axis=-1), jnp.concatenate(parts_s, axis=-1)


def _layout_w_in(w_in):
    depth = w_in.shape[0]
    zeros = lambda w: jnp.zeros((depth, D_MODEL, w), w_in.dtype)
    o = 0
    qa = w_in[..., o:o + Q_LORA]; o += Q_LORA
    kva = w_in[..., o:o + KV_LORA]; o += KV_LORA
    kr = w_in[..., o:o + ROPE_DIM]; o += ROPE_DIM
    rw = w_in[..., o:o + 3 * D_RWKV + 4 * LORA_WA + LORA_G]; o += 3 * D_RWKV + 4 * LORA_WA + LORA_G
    rt = w_in[..., o:]
    kr_blk = lambda x: jnp.concatenate([zeros(ROPE_OFF), x, zeros(HEAD_PAD - ROPE_OFF - ROPE_DIM)], axis=-1)
    rw_parts = [rw[..., :3 * D_RWKV]]
    for j in range(4):
        lo = 3 * D_RWKV + j * LORA_WA
        rw_parts += [rw[..., lo:lo + LORA_WA], zeros(LANE - LORA_WA)]
    rw_parts.append(rw[..., 3 * D_RWKV + 4 * LORA_WA:])
    perm = np.concatenate([h * RET_DK + _rope_perm(RET_DK) for h in range(2 * H_RET)])
    cols = [qa, kva, kr_blk(kr), kr_blk(kr[..., _rope_perm(ROPE_DIM)])] + rw_parts + [rt, rt[..., perm]]
    return jnp.concatenate(cols, axis=-1).astype(BF16)


def _layout_mu(mu):
    depth = mu.shape[0]
    parts = [mu[:, :3 * D_RWKV]]
    for j in range(4):
        lo = 3 * D_RWKV + j * LORA_WA
        parts += [mu[:, lo:lo + LORA_WA], jnp.zeros((depth, LANE - LORA_WA), mu.dtype)]
    parts.append(mu[:, 3 * D_RWKV + 4 * LORA_WA:])
    return jnp.concatenate(parts, axis=-1).reshape(depth, 1, RW_W)


def _layout_wq(wq):
    depth = wq.shape[0]
    w = wq.reshape(depth, Q_LORA, H_MLA, QK_NOPE + ROPE_DIM)
    pad = jnp.zeros((depth, Q_LORA, H_MLA, HEAD_PAD - QK_NOPE - ROPE_DIM), wq.dtype)
    nope, rope = w[..., :QK_NOPE], w[..., QK_NOPE:]
    main = jnp.concatenate([nope, rope, pad], axis=-1)
    swap = jnp.concatenate([jnp.zeros_like(nope), rope[..., _rope_perm(ROPE_DIM)], pad], axis=-1)
    shape = (depth, Q_LORA, H_MLA * HEAD_PAD)
    return main.reshape(shape).astype(BF16), swap.reshape(shape).astype(BF16)


def _layout_wkv(wkv):
    depth = wkv.shape[0]
    w = wkv.reshape(depth, KV_LORA, H_MLA, QK_NOPE + V_HD)
    kn = jnp.concatenate([w[..., :QK_NOPE], jnp.zeros((depth, KV_LORA, H_MLA, HEAD_PAD - QK_NOPE), wkv.dtype)], axis=-1)
    return (kn.reshape(depth, KV_LORA, H_MLA * HEAD_PAD).astype(BF16),
            w[..., QK_NOPE:].reshape(depth, KV_LORA, D_MLA).astype(BF16))


def _layout_lora_up(up):
    depth = up.shape[0]
    return jnp.concatenate([up, jnp.zeros((depth, 4, LANE - LORA_WA, D_RWKV), up.dtype)], axis=2).astype(BF16)


def _block_diag(s):
    eye = jnp.eye(H_RWKV, dtype=s.dtype)
    out = jnp.einsum('...hij,hg->...higj', s, eye)
    return out.reshape(s.shape[:-3] + (STACK, STACK))


def _unpack_state(s):
    nb = s.shape[0]
    return s.reshape(nb, 2, RWKV_HD, H_RWKV, RWKV_HD).transpose(0, 1, 3, 2, 4)


def kernel(x_prompt, x_sample, cache_mla_ckv, cache_mla_krope, state_rwkv, state_ret, c, c_ctx,
           norm_g, w_mod, b_mod, ffn_wi, ffn_wo, w_in, w_out, mla_norm_q, mla_norm_kv, mla_wq_up,
           mla_wkv_up, rwkv_mu, rwkv_vec, rwkv_lora_up, rwkv_g_up, ret_gn):
    bc, nc, _ = x_prompt.shape
    bl, nl, _ = x_sample.shape
    depth = norm_g.shape[0]
    past = cache_mla_ckv.shape[2]
    tc, tl = bc * nc, bl * nl
    assert bl <= 7 and nl % GRID_W == 0
    assert nc % CHUNK == 0 and nl % CHUNK == 0 and tc % nl == 0 and tc % past == 0
    assert bc % PAIR == 0 and bl % PAIR == 0

    def tile(cap):
        tm = cap
        while tc % tm or nl % tm:
            tm //= 2
        return tm

    def grouper(tm):
        ctx_tiles, per_seq = tc // tm, nl // tm
        return lambda i: jnp.where(i < ctx_tiles, 0, 1 + jnp.maximum(i - ctx_tiles, 0) // per_seq)

    tm_ffn, tm_tok = tile(1024), tile(512)
    tm_shift = 256
    while nc % tm_shift or nl % tm_shift:
        tm_shift //= 2
    tq_ctx, tq_lat = min(nc, 256), min(nl, 256)

    wi_b, wo_b = ffn_wi.astype(BF16), ffn_wo.astype(BF16)
    w_in_b = _layout_w_in(w_in)
    w_out_b = w_out.astype(BF16)
    wq_b, wqs_b = _layout_wq(mla_wq_up)
    wk_b, wv_b = _layout_wkv(mla_wkv_up)
    mu_l = _layout_mu(rwkv_mu)
    up_b = _layout_lora_up(rwkv_lora_up)
    gup_b = rwkv_g_up.astype(BF16)

    pos = jnp.arange(nl)
    row_l, col_l = pos // GRID_W, pos % GRID_W
    cos_m, sin_m = _rope_tables(row_l, col_l, ROPE_DIM)
    cos_r, sin_r = _rope_tables(row_l, col_l, RET_DK)

    def mla_table(tab, fill):
        lat = jnp.concatenate([jnp.full((nl, ROPE_OFF), fill, F32), tab,
                               jnp.zeros((nl, HEAD_PAD - ROPE_OFF - ROPE_DIM), F32)], axis=-1)
        ctx = jnp.concatenate([jnp.full((tc, ROPE_OFF + ROPE_DIM), fill, F32),
                               jnp.zeros((tc, HEAD_PAD - ROPE_OFF - ROPE_DIM), F32)], axis=-1)
        return jnp.concatenate([ctx, jnp.tile(lat, (bl, 1))], axis=0)

    def ret_table(tab, fill):
        return jnp.concatenate([jnp.full((tc, D_RET), fill, F32), jnp.tile(jnp.tile(tab, (1, H_RET)), (bl, 1))], axis=0)

    cq, sq = mla_table(cos_m, 1.0), mla_table(sin_m, 0.0)
    cr, sr = ret_table(cos_r, 1.0), ret_table(sin_r, 0.0)

    e = 5.0 + jnp.arange(H_RET, dtype=F32)[None, :] + 0.5 * jnp.arange(2, dtype=F32)[:, None]
    lg = jnp.log1p(-jnp.exp2(-e))
    lg_lane = jnp.repeat(lg, RET_DK, axis=1)
    lgm = jnp.broadcast_to(lg_lane[:, :, None], (2, STACK, STACK))
    lgl = lg_lane[:, None, :]

    c_all = jnp.concatenate([c_ctx[None, :], c, jnp.zeros((7 - bl, D_MODEL), F32)], axis=0)
    mod = _modulation(c_all, w_mod, b_mod).reshape(depth, 8, N_MOD, D_MODEL)

    kr_cache = jnp.pad(cache_mla_krope, ((0, 0), (0, 0), (0, 0), (ROPE_OFF, HEAD_PAD - ROPE_OFF - ROPE_DIM)))
    zero_state = jnp.zeros((bc, 2, STACK, STACK), F32)
    s0_rwkv_lat = _block_diag(state_rwkv.astype(F32))
    s0_ret_lat = _block_diag(state_ret.astype(F32))

    x = jnp.concatenate([x_prompt.reshape(tc, D_MODEL), x_sample.reshape(tl, D_MODEL)], axis=0)
    ckv_l, krope_l, rwkv_l, ret_l = [], [], [], []
    for l in range(depth):
        mod_l, ng = mod[l], norm_g[l]
        x = _ffn(x, mod_l, ng, wi_b, wo_b, l, 0, grouper(tm_ffn), tm_ffn)
        kr, rw, rt, rts, qh, ckv, kh, vv = _mixin(
            x, mod_l, ng, w_in_b[l], cq, sq, mla_norm_q[l][None, :], mla_norm_kv[l][None, :],
            wq_b[l], wqs_b[l], wk_b[l], wv_b[l], grouper(tm_tok), tm_tok)

        kh_c, v_c = _cache_kv(cache_mla_ckv[:, l].reshape(bl * past, KV_LORA),
                              kr_cache[:, l].reshape(bl * past, HEAD_PAD), wk_b[l], wv_b[l])
        mla_ctx = _attention(qh, kh, vv, None, None, nb=bc, n=nc, m1=nc, tok0=0, k1_tok0=0, tq=tq_ctx)
        mla_lat = _attention(qh, kh_c, v_c, kh, vv, nb=bl, n=nl, m1=past, tok0=tc, k1_tok0=0, tq=tq_lat)

        r, kk, v, lw, kka, kd, bonus, gate = _rwkv_prep(
            rw, mu_l[l], rwkv_vec[l], up_b[l], gup_b[l],
            tm=tm_shift, ctx_tiles=tc // tm_shift, ctx_tiles_per_seq=nc // tm_shift,
            lat_tiles_per_seq=nl // tm_shift)
        gn_r = rwkv_vec[l][7:8, :]
        yf_c, yb_c, s_rwkv = _rwkv_scan(r, kk, v, lw, kka, kd, zero_state, gn_r, nb=bc, n=nc, tok0=0)
        yf_l, yb_l, _ = _rwkv_scan(r, kk, v, lw, kka, kd, s0_rwkv_lat[:, l], gn_r, nb=bl, n=nl, tok0=tc)

        gn_t = ret_gn[l][:, None, :]
        of_c, ob_c, s_ret = _retention(rt, rts, cr, sr, lgm, lgl, gn_t, zero_state, nb=bc, n=nc, tok0=0)
        of_l, ob_l, _ = _retention(rt, rts, cr, sr, lgm, lgl, gn_t, s0_ret_lat[:, l], nb=bl, n=nl, tok0=tc)

        x = _mixout(x, mod_l, ng, bonus, gate, w_out_b[l], (mla_ctx, yf_c, yb_c, of_c, ob_c),
                    (mla_lat, yf_l, yb_l, of_l, ob_l), grouper(tm_tok), tm_tok, tc // tm_tok)
        x = _ffn(x, mod_l, ng, wi_b, wo_b, l, 1, grouper(tm_ffn), tm_ffn)

        ckv_l.append(ckv[:tc].reshape(bc, nc, KV_LORA))
        krope_l.append(kr[:tc, ROPE_OFF:ROPE_OFF + ROPE_DIM].reshape(bc, nc, ROPE_DIM))
        rwkv_l.append(_unpack_state(s_rwkv))
        ret_l.append(_unpack_state(s_ret))

    return (x[:tc].reshape(bc, nc, D_MODEL), x[tc:].reshape(bl, nl, D_MODEL),
            jnp.stack(ckv_l, axis=1), jnp.stack(krope_l, axis=1),
            jnp.stack(rwkv_l, axis=1), jnp.stack(ret_l, axis=1))
```
